```python
import jax, jax.numpy as jnp
from jax import lax
import numpy as np

D_MODEL = 1024
BATCH = 8
SEQ = 4096
DEPTH = 2

CTX_LEN = 256
GRID_W = 64
HEAD_DIM = 64
A_WIDTH = D_MODEL // 2
A_Q_HEADS = A_WIDTH // HEAD_DIM
A_KV_HEADS = A_Q_HEADS // 4
B_WIDTH = D_MODEL // 4
B_GROUPS = 4
B_GROUP_DIM = B_WIDTH // B_GROUPS
C_WIDTH = D_MODEL // 4
C_HEADS = C_WIDTH // HEAD_DIM
MIX_WIDTH = A_WIDTH + B_WIDTH + C_WIDTH
IN_WIDTH = A_WIDTH + 2 * A_KV_HEADS * HEAD_DIM + B_WIDTH + 3 * C_WIDTH
NA_WIN_R = 8
NA_WIN_C = 16
Q_BLOCK = 128
ROPE_THETA = 10000.0
N_EXPERTS = 16
EXPERT_FF = 2 * D_MODEL
CAPACITY_FACTOR = 2
EPS = 1e-6

kernel_name = 'hybrid_fourier_gqa_natten_ec_moe_dit'


def rms_norm(x, g):
    xf = x.astype(jnp.float32)
    y = xf * lax.rsqrt(jnp.mean(xf * xf, axis=-1, keepdims=True) + EPS)
    return (y * g.astype(jnp.float32)).astype(x.dtype)


def modulate(h, shift, scale):
    return h * (1 + scale) + shift


def split_heads(t, n_heads):
    b, n, _ = t.shape
    return t.reshape(b, n, n_heads, HEAD_DIM).transpose(0, 2, 1, 3)


def merge_heads(t):
    b, h, n, d = t.shape
    return t.transpose(0, 2, 1, 3).reshape(b, n, h * d)


def axial_rope_angles(n):
    t = jnp.arange(n, dtype=jnp.int32)
    row = (t // GRID_W).astype(jnp.float32)
    col = (t % GRID_W).astype(jnp.float32)
    n_freq = HEAD_DIM // 4
    inv = ROPE_THETA ** (-jnp.arange(n_freq, dtype=jnp.float32) / n_freq)
    ang = jnp.stack([row[:, None] * inv, col[:, None] * inv], axis=1)
    return jnp.cos(ang), jnp.sin(ang)


def apply_axial_rope(x, cos, sin):
    shp = x.shape
    xr = x.astype(jnp.float32).reshape(shp[:-1] + (2, 2, HEAD_DIM // 4))
    x1, x2 = xr[..., 0, :], xr[..., 1, :]
    out = jnp.stack([x1 * cos - x2 * sin, x1 * sin + x2 * cos], axis=-2)
    return out.reshape(shp).astype(x.dtype)


def gqa_dense(q, k, v):
    b, hq, n, d = q.shape
    hkv = k.shape[1]
    qg = q.reshape(b, hkv, hq // hkv, n, d)
    s = jnp.einsum('bkgnd,bkmd->bkgnm', qg, k, preferred_element_type=jnp.float32) * (d ** -0.5)
    p = jax.nn.softmax(s, axis=-1).astype(v.dtype)
    o = jnp.einsum('bkgnm,bkmd->bkgnd', p, v)
    return o.reshape(b, hq, n, d)


def gqa_blocked(q, k, v):
    b, hq, n, d = q.shape
    nb = n // Q_BLOCK
    qb = q.reshape(b, hq, nb, Q_BLOCK, d).transpose(2, 0, 1, 3, 4)
    out = lax.map(lambda qi: gqa_dense(qi, k, v), qb)
    return out.transpose(1, 2, 0, 3, 4).reshape(b, hq, n, d)


def fourier_mix(u):
    b, n, _ = u.shape
    ug = u.astype(jnp.float32).reshape(b, n, B_GROUPS, B_GROUP_DIM)
    f = jnp.fft.fft2(ug, axes=(1, 3), norm='ortho').real
    return f.reshape(b, n, B_WIDTH).astype(u.dtype)


def neighbourhood_attention(q, k, v, kc, vc, rel_bias):
    b, h, n, d = q.shape
    rows = n // GRID_W
    wr = min(NA_WIN_R, rows)
    wc = min(NA_WIN_C, GRID_W)
    rpb = Q_BLOCK // GRID_W
    nb = rows // rpb
    kg = k.reshape(b, h, rows, GRID_W, d)
    vg = v.reshape(b, h, rows, GRID_W, d)
    cols = jnp.arange(GRID_W)
    cs = jnp.clip(cols - wc // 2, 0, GRID_W - wc)
    col_idx = cs[:, None] + jnp.arange(wc)[None, :]
    col_off = col_idx - cols[:, None] + (NA_WIN_C - 1)
    qb = q.reshape(b, h, nb, rpb, GRID_W, d).transpose(2, 0, 1, 3, 4, 5)
    scale = d ** -0.5
    nwin = wr * wc

    def block(args):
        i, qi = args
        r = i * rpb + jnp.arange(rpb)
        rs = jnp.clip(r - wr // 2, 0, rows - wr)
        row_idx = rs[:, None] + jnp.arange(wr)[None, :]
        row_off = row_idx - r[:, None] + (NA_WIN_R - 1)
        k_win = jnp.take(jnp.take(kg, row_idx, axis=2), col_idx, axis=4)
        v_win = jnp.take(jnp.take(vg, row_idx, axis=2), col_idx, axis=4)
        s_win = jnp.einsum('bhrcd,bhricjd->bhrcij', qi, k_win, preferred_element_type=jnp.float32) * scale
        bias = rel_bias[:, row_off[:, None, :, None], col_off[None, :, None, :]]
        s_win = s_win + bias.astype(jnp.float32)[None]
        s_ctx = jnp.einsum('bhrcd,bhld->bhrcl', qi, kc, preferred_element_type=jnp.float32) * scale
        s = jnp.concatenate([s_win.reshape(b, h, rpb, GRID_W, nwin), s_ctx], axis=-1)
        p = jax.nn.softmax(s, axis=-1).astype(v.dtype)
        p_win = p[..., :nwin].reshape(b, h, rpb, GRID_W, wr, wc)
        p_ctx = p[..., nwin:]
        return (jnp.einsum('bhrcij,bhricjd->bhrcd', p_win, v_win)
                + jnp.einsum('bhrcl,bhld->bhrcd', p_ctx, vc))

    out = lax.map(block, (jnp.arange(nb), qb))
    return out.transpose(1, 2, 0, 3, 4, 5).reshape(b, h, n, d)


def merge_groups(o_a, o_b, o_c, g_out_a, g_out_b, g_out_c, w_out):
    y = jnp.concatenate([rms_norm(o_a, g_out_a), rms_norm(o_b, g_out_b), rms_norm(o_c, g_out_c)], axis=-1)
    return y @ w_out


def mixer_sublayer(hl, hc, w_in, g_q, g_k, rel_bias, g_out_a, g_out_b, g_out_c, w_out, cos, sin, ctx_out):
    kv = A_KV_HEADS * HEAD_DIM
    splits = np.cumsum([A_WIDTH, kv, kv, B_WIDTH, C_WIDTH, C_WIDTH]).tolist()
    qa, ka, va, ub, qn, kn, vn = jnp.split(hl @ w_in, splits, axis=-1)
    qa_c, ka_c, va_c, ub_c, qn_c, kn_c, vn_c = jnp.split(hc @ w_in, splits, axis=-1)
    ka_c = rms_norm(split_heads(ka_c, A_KV_HEADS), g_k)
    va_c = split_heads(va_c, A_KV_HEADS)
    kn_c = split_heads(kn_c, C_HEADS)
    vn_c = split_heads(vn_c, C_HEADS)
    qa = apply_axial_rope(rms_norm(split_heads(qa, A_Q_HEADS), g_q), cos, sin)
    ka = apply_axial_rope(rms_norm(split_heads(ka, A_KV_HEADS), g_k), cos, sin)
    va = split_heads(va, A_KV_HEADS)
    o_a = gqa_blocked(qa, jnp.concatenate([ka, ka_c], axis=2), jnp.concatenate([va, va_c], axis=2))
    o_b = fourier_mix(ub)
    o_c = neighbourhood_attention(split_heads(qn, C_HEADS), split_heads(kn, C_HEADS), split_heads(vn, C_HEADS),
                                  kn_c, vn_c, rel_bias)
    yl = merge_groups(merge_heads(o_a), o_b, merge_heads(o_c), g_out_a, g_out_b, g_out_c, w_out)
    if not ctx_out:
        return yl, None
    o_a_c = gqa_dense(rms_norm(split_heads(qa_c, A_Q_HEADS), g_q), ka_c, va_c)
    o_b_c = fourier_mix(ub_c)
    o_c_c = gqa_dense(split_heads(qn_c, C_HEADS), kn_c, vn_c)
    yc = merge_groups(merge_heads(o_a_c), o_b_c, merge_heads(o_c_c), g_out_a, g_out_b, g_out_c, w_out)
    return yl, yc


def expert_choice_moe(h, w_router, w_gate, w_up, w_down):
    b, n, d = h.shape
    cap = max(1, CAPACITY_FACTOR * n // N_EXPERTS)
    logits = jnp.einsum('bnd,de->ben', h, w_router, preferred_element_type=jnp.float32)
    aff = jax.nn.softmax(logits, axis=1)
    gate, idx = lax.top_k(aff, cap)
    xe = jax.vmap(lambda hb, ib: hb[ib])(h, idx)
    g = jnp.einsum('becd,edf->becf', xe, w_gate)
    u = jnp.einsum('becd,edf->becf', xe, w_up)
    y = jnp.einsum('becf,efd->becd', jax.nn.silu(g) * u, w_down) * gate[..., None].astype(h.dtype)

    def scatter(ib, yb):
        return jnp.zeros((n, d), h.dtype).at[ib.reshape(-1)].add(yb.reshape(-1, d))

    return jax.vmap(scatter)(idx, y)


def setup_inputs(seed: int = 0) -> dict:
    key = jax.random.key(seed)
    ks = jax.random.split(key, 24)
    D = D_MODEL
    nrm = jax.random.normal

    def gain(k, shape):
        return 1.0 + 0.02 * nrm(k, shape, jnp.float32)

    return {
        'x': nrm(ks[0], (BATCH, SEQ, D), jnp.float32),
        'c': nrm(ks[1], (BATCH, D), jnp.float32),
        'ctx': nrm(ks[2], (BATCH, CTX_LEN, D), jnp.float32),
        'c_ctx': nrm(ks[3], (D,), jnp.float32),
        'w_ada': nrm(ks[4], (DEPTH, D, 6 * D), jnp.float32) * D ** -0.5,
        'b_ada': 0.01 * nrm(ks[5], (DEPTH, 6 * D), jnp.float32),
        'g_mix': gain(ks[6], (DEPTH, D)),
        'g_ffn': gain(ks[7], (DEPTH, D)),
        'w_in': nrm(ks[8], (DEPTH, D, IN_WIDTH), jnp.float32) * D ** -0.5,
        'g_q': gain(ks[9], (DEPTH, HEAD_DIM)),
        'g_k': gain(ks[10], (DEPTH, HEAD_DIM)),
        'rel_bias': 0.1 * nrm(ks[11], (DEPTH, C_HEADS, 2 * NA_WIN_R - 1, 2 * NA_WIN_C - 1), jnp.float32),
        'g_out_a': gain(ks[12], (DEPTH, A_WIDTH)),
        'g_out_b': gain(ks[13], (DEPTH, B_WIDTH)),
        'g_out_c': gain(ks[14], (DEPTH, C_WIDTH)),
        'w_out': nrm(ks[15], (DEPTH, MIX_WIDTH, D), jnp.float32) * MIX_WIDTH ** -0.5,
        'w_router': nrm(ks[16], (DEPTH, D, N_EXPERTS), jnp.float32) * D ** -0.5,
        'w_gate': nrm(ks[17], (DEPTH, N_EXPERTS, D, EXPERT_FF), jnp.float32) * D ** -0.5,
        'w_up': nrm(ks[18], (DEPTH, N_EXPERTS, D, EXPERT_FF), jnp.float32) * D ** -0.5,
        'w_down': nrm(ks[19], (DEPTH, N_EXPERTS, EXPERT_FF, D), jnp.float32) * EXPERT_FF ** -0.5,
        'g_final': gain(ks[20], (D,)),
    }


def reference(x, c, ctx, c_ctx, w_ada, b_ada, g_mix, g_ffn, w_in, g_q, g_k, rel_bias, g_out_a, g_out_b,
              g_out_c, w_out, w_router, w_gate, w_up, w_down, g_final):
    n = x.shape[1]
    cos, sin = axial_rope_angles(n)
    sc = jax.nn.silu(c)
    sx = jax.nn.silu(c_ctx)
    xl, xc = x, ctx
    for l in range(DEPTH):
        last = l == DEPTH - 1
        ml = [m[:, None, :] for m in jnp.split(sc @ w_ada[l] + b_ada[l], 6, axis=-1)]
        mc = jnp.split(sx @ w_ada[l] + b_ada[l], 6, axis=-1)
        hl = modulate(rms_norm(xl, g_mix[l]), ml[0], ml[1])
        hc = modulate(rms_norm(xc, g_mix[l]), mc[0], mc[1])
        yl, yc = mixer_sublayer(hl, hc, w_in[l], g_q[l], g_k[l], rel_bias[l], g_out_a[l], g_out_b[l],
                                g_out_c[l], w_out[l], cos, sin, not last)
        xl = xl + ml[2] * yl
        hl = modulate(rms_norm(xl, g_ffn[l]), ml[3], ml[4])
        xl = xl + ml[5] * expert_choice_moe(hl, w_router[l], w_gate[l], w_up[l], w_down[l])
        if not last:
            xc = xc + mc[2] * yc
            hc = modulate(rms_norm(xc, g_ffn[l]), mc[3], mc[4])
            xc = xc + mc[5] * expert_choice_moe(hc, w_router[l], w_gate[l], w_up[l], w_down[l])
    return rms_norm(xl, g_final)
```

```python
import functools
import math

import numpy as np
import jax
import jax.numpy as jnp
from jax import lax
from jax.experimental import pallas as pl
from jax.experimental.pallas import tpu as pltpu

F32 = jnp.float32
BF16 = jnp.bfloat16
HIGHEST = lax.Precision.HIGHEST

GRID_W = 64
HEAD_DIM = 64
A_Q_HEADS = 8
A_KV_HEADS = 2
B_GROUPS = 4
C_HEADS = 4
NA_WIN_R = 8
NA_WIN_C = 16
ROPE_THETA = 10000.0
N_EXPERTS = 16
CAPACITY_FACTOR = 2
EPS = 1e-6
MOD_ROWS = 16

LANES = 128
SUBLANES = 8
VMEM_BYTES_V7X = 64 * 1024 * 1024

ROW_TILE = 512
ATTN_Q_TILE = 256
FOURIER_ROW_TILE = 512
NA_Q_ROWS = 4
FFN_ROW_TILE = 512
FFN_F_CHUNK = 512
NEG_BIG = -1e30


def _vmem_limit(nbytes):
    return int(min(max(2 * nbytes, 32 * 1024 * 1024), VMEM_BYTES_V7X - 8 * 1024 * 1024))


def _params(sem, nbytes):
    return pltpu.CompilerParams(dimension_semantics=sem, vmem_limit_bytes=_vmem_limit(nbytes))


def _rms(x, g):
    return x * lax.rsqrt(jnp.mean(x * x, axis=-1, keepdims=True) + EPS) * g


def _ada_kernel(c_ref, w_ref, b_ref, o_ref):
    c = c_ref[...]
    sc = c / (1.0 + jnp.exp(-c))
    o_ref[...] = jnp.dot(sc, w_ref[...], precision=HIGHEST, preferred_element_type=F32) + b_ref[...]


def _ada(cvec, w_ada, b_ada):
    depth, d, n6 = w_ada.shape
    tn = 1024
    return pl.pallas_call(
        _ada_kernel,
        grid=(depth, n6 // tn),
        in_specs=[
            pl.BlockSpec((MOD_ROWS, d), lambda l, j: (0, 0)),
            pl.BlockSpec((None, d, tn), lambda l, j: (l, 0, j)),
            pl.BlockSpec((None, 1, tn), lambda l, j: (l, 0, j)),
        ],
        out_specs=pl.BlockSpec((None, MOD_ROWS, tn), lambda l, j: (l, 0, j)),
        out_shape=jax.ShapeDtypeStruct((depth, MOD_ROWS, n6), F32),
        compiler_params=_params(("arbitrary", "arbitrary"), 2 * d * tn * 4),
        name="ada_mod",
    )(cvec, w_ada, b_ada.reshape(depth, 1, n6))


def _head_norm(t, ones, g):
    ssq = jnp.dot((t * t).astype(BF16), ones, preferred_element_type=F32)
    return t * lax.rsqrt(ssq * (1.0 / HEAD_DIM) + EPS) * g


def _rope(t, cos, sin):
    lane = lax.broadcasted_iota(jnp.int32, t.shape, 1)
    first = (lane & 31) < 16
    partner = jnp.where(first, pltpu.roll(t, LANES - 16, 1), pltpu.roll(t, 16, 1))
    return t * cos + partner * sin


def _inproj_kernel(*refs, rope, widths):
    a_w, kv_w, b_w, c_w = widths
    if rope:
        (x_ref, g_ref, shift_ref, scale_ref, w_ref, gq_ref, gk_ref, ones_ref, cs_ref, cos_ref, sin_ref,
         qa_ref, ka_ref, va_ref, yc_ref, ys_ref, qn_ref, kn_ref, vn_ref) = refs
        cos, sin = cos_ref[...], sin_ref[...]
    else:
        (x_ref, g_ref, shift_ref, scale_ref, w_ref, gq_ref, gk_ref, ones_ref, cs_ref,
         qa_ref, ka_ref, va_ref, yc_ref, ys_ref, qn_ref, kn_ref, vn_ref, vnt_ref) = refs
        cos = sin = None
    h = _rms(x_ref[...], g_ref[...]) * (1.0 + scale_ref[...]) + shift_ref[...]
    p = jnp.dot(h.astype(BF16), w_ref[...], preferred_element_type=F32)
    ones = ones_ref[...]
    q_scale = HEAD_DIM ** -0.5
    for j in range(a_w // LANES):
        t = _head_norm(p[:, j * LANES:(j + 1) * LANES], ones, gq_ref[...])
        if rope:
            t = _rope(t, cos, sin)
        qa_ref[:, j * LANES:(j + 1) * LANES] = (t * q_scale).astype(BF16)
    o = a_w
    t = _head_norm(p[:, o:o + kv_w], ones, gk_ref[...])
    if rope:
        t = _rope(t, cos, sin)
    ka_ref[...] = t.astype(BF16)
    o += kv_w
    va_ref[...] = jnp.transpose(p[:, o:o + kv_w]).astype(BF16)
    o += kv_w
    y = jnp.dot(p[:, o:o + b_w].astype(BF16), cs_ref[...], preferred_element_type=F32)
    yc_ref[...] = y[:, :b_w].astype(BF16)
    ys_ref[...] = y[:, b_w:].astype(BF16)
    o += b_w
    qn_ref[...] = (p[:, o:o + c_w] * q_scale).astype(BF16)
    o += c_w
    kn_ref[...] = p[:, o:o + c_w].astype(BF16)
    o += c_w
    vn_ref[...] = p[:, o:o + c_w].astype(BF16)
    if not rope:
        vnt_ref[...] = jnp.transpose(p[:, o:o + c_w]).astype(BF16)


def _inproj(x2, mod, mod_row, g_mix, w_in_bf, gq_t, gk_t, ones_hd, cs, rope_tabs, seq, tm):
    rows, d = x2.shape
    in_w = w_in_bf.shape[1]
    a_w = A_Q_HEADS * HEAD_DIM
    kv_w = A_KV_HEADS * HEAD_DIM
    b_w = cs.shape[0]
    c_w = C_HEADS * HEAD_DIM
    assert kv_w == LANES and in_w == a_w + 2 * kv_w + b_w + 3 * c_w
    rope = rope_tabs is not None
    steps_per_seq = seq // tm
    const = lambda i: (0, 0)
    row = lambda i: (i, 0)
    in_specs = [
        pl.BlockSpec((tm, d), row),
        pl.BlockSpec((1, d), const),
        pl.BlockSpec((None, None, 1, d), lambda i: (mod_row(i), 0, 0, 0)),
        pl.BlockSpec((None, None, 1, d), lambda i: (mod_row(i), 1, 0, 0)),
        pl.BlockSpec((d, in_w), const),
        pl.BlockSpec((1, LANES), const),
        pl.BlockSpec((1, LANES), const),
        pl.BlockSpec((LANES, LANES), const),
        pl.BlockSpec((b_w, 2 * b_w), const),
    ]
    args = [x2, g_mix.reshape(1, d), mod, mod, w_in_bf, gq_t, gk_t, ones_hd, cs]
    if rope:
        in_specs += [pl.BlockSpec((tm, LANES), lambda i: (i % steps_per_seq, 0))] * 2
        args += list(rope_tabs)
    out_w = [a_w, kv_w, kv_w, b_w, b_w, c_w, c_w, c_w]
    out_specs = [pl.BlockSpec((tm, w), row) for w in out_w]
    out_shape = [jax.ShapeDtypeStruct((rows, w), BF16) for w in out_w]
    out_specs[2] = pl.BlockSpec((kv_w, tm), lambda i: (0, i))
    out_shape[2] = jax.ShapeDtypeStruct((kv_w, rows), BF16)
    if not rope:
        out_specs.append(pl.BlockSpec((c_w, tm), lambda i: (0, i)))
        out_shape.append(jax.ShapeDtypeStruct((c_w, rows), BF16))
    return pl.pallas_call(
        functools.partial(_inproj_kernel, rope=rope, widths=(a_w, kv_w, b_w, c_w)),
        grid=(rows // tm,),
        in_specs=in_specs,
        out_specs=out_specs,
        out_shape=out_shape,
        compiler_params=_params(("parallel",), 2 * d * in_w * 2 + 2 * tm * d * 4 + 3 * tm * in_w * 4),
        name="inproj_rope" if rope else "inproj",
    )(*args)


def _place_head(q_ref, h, kv, kw):
    assert 2 * HEAD_DIM == LANES
    src_blk, src_off = divmod(h * HEAD_DIM, LANES)
    dst_blk, dst_off = divmod(kv * HEAD_DIM, LANES)
    blk = q_ref[:, src_blk * LANES:(src_blk + 1) * LANES].astype(F32)
    if src_off != dst_off:
        blk = pltpu.roll(blk, HEAD_DIM, 1)
    lane = lax.broadcasted_iota(jnp.int32, blk.shape, 1)
    keep = (lane >= dst_off) & (lane < dst_off + HEAD_DIM)
    blk = jnp.where(keep, blk, 0.0).astype(BF16)
    parts = [blk if j == dst_blk else jnp.zeros_like(blk) for j in range(kw // LANES)]
    return parts[0] if len(parts) == 1 else jnp.concatenate(parts, axis=1)


def _attn_kernel(*refs, n_kv, group, two):
    if two:
        q_ref, k1_ref, v1_ref, k2_ref, v2_ref, o_ref, s_scr, p_scr = refs
    else:
        q_ref, k1_ref, v1_ref, o_ref, s_scr, p_scr = refs
    n1, kw = k1_ref.shape
    n_keys = s_scr.shape[1]
    dn = (((1,), (1,)), ((), ()))
    sum_rows = 2 * SUBLANES
    ones = jnp.ones((sum_rows, n_keys), BF16)
    n_heads = n_kv * group

    def scores(h):
        slot = h % 2
        qp = _place_head(q_ref, h, h // group, kw)
        s = lax.dot_general(k1_ref[...], qp, dn, preferred_element_type=F32)
        s_scr[slot, 0:n1, :] = s
        m = jnp.max(s, axis=0, keepdims=True)
        if two:
            s = lax.dot_general(k2_ref[...], qp, dn, preferred_element_type=F32)
            s_scr[slot, n1:n_keys, :] = s
            m = jnp.maximum(m, jnp.max(s, axis=0, keepdims=True))
        return m

    def finish(h, m):
        slot, kv = h % 2, h // group
        vs = slice(kv * HEAD_DIM, (kv + 1) * HEAD_DIM)
        v = jnp.concatenate([v1_ref[vs, :], v2_ref[vs, :]], axis=1) if two else v1_ref[vs, :]
        va = jnp.concatenate([v, ones], axis=0)
        p_scr[slot] = jnp.exp(s_scr[slot] - m).astype(BF16)
        acc = jnp.dot(va, p_scr[slot], preferred_element_type=F32)
        o = acc[:HEAD_DIM] * (1.0 / acc[HEAD_DIM:HEAD_DIM + 1])
        o_ref[h * HEAD_DIM:(h + 1) * HEAD_DIM, :] = o.astype(BF16)

    m_next = scores(0)
    for h in range(n_heads):
        m_cur = m_next
        if h + 1 < n_heads:
            m_next = scores(h + 1)
        finish(h, m_cur)


def _attention(q, k1, v1t, k2, v2t, batch, n_kv, group, tq):
    rows, qw = q.shape
    n_q = rows // batch
    n1 = k1.shape[0] // batch
    kw = k1.shape[1]
    two = k2 is not None
    steps = n_q // tq
    in_specs = [
        pl.BlockSpec((tq, qw), lambda b, i: (b * steps + i, 0)),
        pl.BlockSpec((n1, kw), lambda b, i: (b, 0)),
        pl.BlockSpec((kw, n1), lambda b, i: (0, b)),
    ]
    args = [q, k1, v1t]
    n_keys = n1
    if two:
        n2 = k2.shape[0] // batch
        in_specs += [pl.BlockSpec((n2, kw), lambda b, i: (b, 0)), pl.BlockSpec((kw, n2), lambda b, i: (0, b))]
        args += [k2, v2t]
        n_keys += n2
    return pl.pallas_call(
        functools.partial(_attn_kernel, n_kv=n_kv, group=group, two=two),
        grid=(batch, steps),
        in_specs=in_specs,
        out_specs=pl.BlockSpec((qw, tq), lambda b, i: (0, b * steps + i)),
        out_shape=jax.ShapeDtypeStruct((qw, rows), BF16),
        scratch_shapes=[pltpu.VMEM((2, n_keys, tq), F32), pltpu.VMEM((2, n_keys, tq), BF16)],
        compiler_params=_params(("parallel", "arbitrary"), 4 * n_keys * kw * 2 + 4 * tq * n_keys * 4),
        name="attn_two_seg" if two else "attn_one_seg",
    )(*args)


def _fourier_kernel(c_ref, s_ref, yc_ref, ys_ref, o_ref):
    o = jnp.dot(c_ref[...], yc_ref[...], preferred_element_type=F32)
    o = o + jnp.dot(s_ref[...], ys_ref[...], preferred_element_type=F32)
    o_ref[...] = o.astype(BF16)


def _fourier(c_tab, s_tab, yc, ys, batch, tm):
    n = c_tab.shape[0]
    w = yc.shape[1]
    steps = n // tm
    return pl.pallas_call(
        _fourier_kernel,
        grid=(steps, batch),
        in_specs=[
            pl.BlockSpec((tm, n), lambda i, b: (i, 0)),
            pl.BlockSpec((tm, n), lambda i, b: (i, 0)),
            pl.BlockSpec((n, w), lambda i, b: (b, 0)),
            pl.BlockSpec((n, w), lambda i, b: (b, 0)),
        ],
        out_specs=pl.BlockSpec((tm, w), lambda i, b: (b * steps + i, 0)),
        out_shape=jax.ShapeDtypeStruct((batch * n, w), BF16),
        compiler_params=_params(("arbitrary", "arbitrary"), 4 * tm * n * 2 + 4 * n * w * 2),
        name="fourier",
    )(c_tab, s_tab, yc, ys)


def _dft_tables(n, group_dim):
    scale = 1.0 / math.sqrt(n * group_dim)
    r = int(round(math.sqrt(n)))
    if r * r != n:
        k = jnp.arange(n, dtype=jnp.int32)
        ang = ((k[:, None] * k[None, :]) % n).astype(F32) * (2.0 * math.pi / n)
        return (jnp.cos(ang) * scale).astype(BF16), (-jnp.sin(ang) * scale).astype(BF16)
    k = jnp.arange(n, dtype=jnp.int32)[:, None]
    t = jnp.arange(r, dtype=jnp.int32)[None, :]
    ang_a = ((k * t) % r).astype(F32) * (2.0 * math.pi / r)
    ang_b = ((k * t) % n).astype(F32) * (2.0 * math.pi / n)
    ca, sa = jnp.cos(ang_a)[:, :, None], jnp.sin(ang_a)[:, :, None]
    cb, sb = jnp.cos(ang_b)[:, None, :], jnp.sin(ang_b)[:, None, :]
    c = (ca * cb - sa * sb) * scale
    s = (sa * cb + ca * sb) * (-scale)
    return c.reshape(n, n).astype(BF16), s.reshape(n, n).astype(BF16)


def _channel_dft(width, group_dim):
    j = np.arange(group_dim)
    ang = 2.0 * np.pi * ((j[:, None] * j[None, :]) % group_dim) / group_dim
    eye = np.eye(width // group_dim)
    cs = np.concatenate([np.kron(eye, np.cos(ang)), np.kron(eye, np.sin(ang))], axis=1)
    return jnp.asarray(cs, dtype=F32).astype(BF16)


def _na_kernel(ws_ref, var_ref, q_ref, k_ref, v_ref, kc_ref, vc_ref, bias_ref, o_ref, *, band):
    i = pl.program_id(1)
    start = pl.multiple_of(ws_ref[i] * GRID_W, GRID_W)
    var = var_ref[i]
    kw = k_ref[pl.ds(start, band), :]
    vw = v_ref[pl.ds(start, band), :]
    dn = (((1,), (1,)), ((), ()))
    for h in range(C_HEADS):
        hs = slice(h * HEAD_DIM, (h + 1) * HEAD_DIM)
        q = q_ref[:, hs]
        s1 = lax.dot_general(q, kw[:, hs], dn, preferred_element_type=F32) + bias_ref[var, h]
        s2 = lax.dot_general(q, kc_ref[:, hs], dn, preferred_element_type=F32)
        m = jnp.maximum(jnp.max(s1, axis=-1, keepdims=True), jnp.max(s2, axis=-1, keepdims=True))
        p1 = jnp.exp(s1 - m)
        p2 = jnp.exp(s2 - m)
        l = jnp.sum(p1, axis=-1, keepdims=True) + jnp.sum(p2, axis=-1, keepdims=True)
        o = jnp.dot(p1.astype(BF16), vw[:, hs], preferred_element_type=F32)
        o = o + jnp.dot(p2.astype(BF16), vc_ref[:, hs], preferred_element_type=F32)
        o_ref[:, hs] = (o / l).astype(BF16)


def _na_plan(seq):
    rows = seq // GRID_W
    wr = min(NA_WIN_R, rows)
    wc = min(NA_WIN_C, GRID_W)
    qr = min(NA_Q_ROWS, rows)
    band_rows = min(rows, 2 * ((qr + wr) // 2))
    nblk = rows // qr
    n_row_off, n_col_off = 2 * NA_WIN_R - 1, 2 * NA_WIN_C - 1
    c = np.arange(GRID_W)
    cs = np.clip(c - wc // 2, 0, GRID_W - wc)
    col_ok = (c[None, :] >= cs[:, None]) & (c[None, :] < cs[:, None] + wc)
    col_idx = np.where(col_ok, c[None, :] - c[:, None] + (NA_WIN_C - 1), n_col_off).astype(np.int32)
    ws_list, var_list, variants = [], [], []
    for blk in range(nblk):
        r = blk * qr + np.arange(qr)
        rs = np.clip(r - wr // 2, 0, rows - wr)
        ws = int(np.clip(rs[0], 0, rows - band_rows))
        kr = ws + np.arange(band_rows)
        row_ok = (kr[None, :] >= rs[:, None]) & (kr[None, :] < rs[:, None] + wr)
        assert row_ok.sum(axis=1).min() == wr
        idx = np.where(row_ok, kr[None, :] - r[:, None] + (NA_WIN_R - 1), n_row_off).astype(np.int32)
        for v, known in enumerate(variants):
            if np.array_equal(known, idx):
                break
        else:
            v = len(variants)
            variants.append(idx)
        ws_list.append(ws)
        var_list.append(v)
    return (np.asarray(ws_list, np.int32), np.asarray(var_list, np.int32),
            np.stack(variants), col_idx, qr, band_rows)


def _natten(qn, kn, vn, kc, vc, rel_bias_l, batch):
    rows, w = qn.shape
    seq = rows // batch
    ctx_len = kc.shape[0] // batch
    ws, var, row_idx, col_idx, qr, band_rows = _na_plan(seq)
    nblk = ws.shape[0]
    tq, band = qr * GRID_W, band_rows * GRID_W
    nvar = row_idx.shape[0]
    padded = jnp.pad(rel_bias_l, ((0, 0), (0, 1), (0, 1)), constant_values=NEG_BIG)
    tiles = padded[:, :, col_idx]
    bias = jnp.take(tiles, jnp.asarray(row_idx.reshape(-1)), axis=1)
    bias = bias.reshape(C_HEADS, nvar, qr, band_rows, GRID_W, GRID_W)
    bias = jnp.transpose(bias, (1, 0, 2, 4, 3, 5)).reshape(nvar, C_HEADS, tq, band)
    grid_spec = pltpu.PrefetchScalarGridSpec(
        num_scalar_prefetch=2,
        grid=(batch, nblk),
        in_specs=[
            pl.BlockSpec((tq, w), lambda b, i, ws_r, var_r: (b * nblk + i, 0)),
            pl.BlockSpec((seq, w), lambda b, i, ws_r, var_r: (b, 0)),
            pl.BlockSpec((seq, w), lambda b, i, ws_r, var_r: (b, 0)),
            pl.BlockSpec((ctx_len, w), lambda b, i, ws_r, var_r: (b, 0)),
            pl.BlockSpec((ctx_len, w), lambda b, i, ws_r, var_r: (b, 0)),
            pl.BlockSpec((nvar, C_HEADS, tq, band), lambda b, i, ws_r, var_r: (0, 0, 0, 0)),
        ],
        out_specs=pl.BlockSpec((tq, w), lambda b, i, ws_r, var_r: (b * nblk + i, 0)),
    )
    nbytes = 2 * nvar * C_HEADS * tq * band * 4 + 4 * seq * w * 2 + 6 * tq * (band + ctx_len) * 4
    return pl.pallas_call(
        functools.partial(_na_kernel, band=band),
        grid_spec=grid_spec,
        out_shape=jax.ShapeDtypeStruct((rows, w), BF16),
        compiler_params=_params(("parallel", "arbitrary"), nbytes),
        name="natten",
    )(jnp.asarray(ws), jnp.asarray(var), qn, kn, vn, kc, vc, bias)


def _merge_kernel(oa_ref, ob_ref, oc_ref, x_ref, gate_ref, shift_ref, scale_ref, ga_ref, gb_ref, gc_ref,
                  w_ref, gf_ref, wr_ref, xo_ref, h_ref, lg_ref, *, c_feature_major):
    def normed(o_ref, g_ref):
        return _rms(o_ref[...].astype(F32), g_ref[...]).astype(BF16)

    def normed_t(o_ref, g_ref):
        o = o_ref[...].astype(F32)
        return (o * lax.rsqrt(jnp.mean(o * o, axis=0, keepdims=True) + EPS) * g_ref[...]).astype(BF16)

    dn_t = (((0,), (0,)), ((), ()))
    a_w, b_w = oa_ref.shape[0], ob_ref.shape[1]
    y = lax.dot_general(normed_t(oa_ref, ga_ref), w_ref[0:a_w, :], dn_t, preferred_element_type=F32)
    y = y + jnp.dot(normed(ob_ref, gb_ref), w_ref[a_w:a_w + b_w, :], preferred_element_type=F32)
    if c_feature_major:
        y = y + lax.dot_general(normed_t(oc_ref, gc_ref), w_ref[a_w + b_w:, :], dn_t,
                                preferred_element_type=F32)
    else:
        y = y + jnp.dot(normed(oc_ref, gc_ref), w_ref[a_w + b_w:, :], preferred_element_type=F32)
    x = x_ref[...] + gate_ref[...] * y
    xo_ref[...] = x
    h = _rms(x, gf_ref[...]) * (1.0 + scale_ref[...]) + shift_ref[...]
    h_ref[...] = h
    lg_ref[...] = lax.dot_general(wr_ref[...], h, (((1,), (1,)), ((), ())), precision=HIGHEST,
                                  preferred_element_type=F32)


def _merge(oa_t, ob, oc, c_feature_major, x2, mod, mod_row, g_a, g_b, g_c, w_out_bf, g_ffn, w_router_t, tm):
    rows, d = x2.shape
    a_w, b_w = oa_t.shape[0], ob.shape[1]
    row = lambda i: (i, 0)
    col = lambda i: (0, i)
    const = lambda i: (0, 0)
    mod_spec = lambda j: pl.BlockSpec((None, None, 1, d), lambda i: (mod_row(i), j, 0, 0))
    if c_feature_major:
        c_w = oc.shape[0]
        oc_spec, gc_spec, g_c2 = pl.BlockSpec((c_w, tm), col), pl.BlockSpec((c_w, 1), const), g_c.reshape(-1, 1)
    else:
        c_w = oc.shape[1]
        oc_spec, gc_spec, g_c2 = pl.BlockSpec((tm, c_w), row), pl.BlockSpec((1, c_w), const), g_c.reshape(1, -1)
    return pl.pallas_call(
        functools.partial(_merge_kernel, c_feature_major=c_feature_major),
        grid=(rows // tm,),
        in_specs=[
            pl.BlockSpec((a_w, tm), col), pl.BlockSpec((tm, b_w), row), oc_spec,
            pl.BlockSpec((tm, d), row),
            mod_spec(2), mod_spec(3), mod_spec(4),
            pl.BlockSpec((a_w, 1), const), pl.BlockSpec((1, b_w), const), gc_spec,
            pl.BlockSpec((a_w + b_w + c_w, d), const),
            pl.BlockSpec((1, d), const),
            pl.BlockSpec((N_EXPERTS, d), const),
        ],
        out_specs=[pl.BlockSpec((tm, d), row), pl.BlockSpec((tm, d), row),
                   pl.BlockSpec((N_EXPERTS, tm), lambda i: (0, i))],
        out_shape=[jax.ShapeDtypeStruct((rows, d), F32), jax.ShapeDtypeStruct((rows, d), F32),
                   jax.ShapeDtypeStruct((N_EXPERTS, rows), F32)],
        compiler_params=_params(("parallel",), 2 * d * d * 2 + 8 * tm * d * 4),
        name="merge_out",
    )(oa_t, ob, oc, x2, mod, mod, mod, g_a.reshape(-1, 1), g_b.reshape(1, -1), g_c2,
      w_out_bf, g_ffn.reshape(1, d), w_router_t)


def _route_kernel(lg_ref, tri_ref, idx_ref, gate_ref, aff_scr, pos_scr, *, n, cap):
    lg = lg_ref[...]
    e = jnp.exp(lg - jnp.max(lg, axis=0, keepdims=True))
    aff = e / jnp.sum(e, axis=0, keepdims=True)
    fcap = float(cap)

    def enough(v):
        return jnp.sum(jnp.where(aff >= v, 1.0, 0.0), axis=1, keepdims=True) >= fcap

    p = jnp.full((N_EXPERTS, 1), 2.0, F32)
    for j in (64, 32, 16, 8, 4, 2, 1):
        cand = p * (2.0 ** -j)
        p = jnp.where(enough(cand), p, cand)
    base = p * 0.5
    mant = jnp.zeros((N_EXPERTS, 1), F32)
    for j in range(22, -1, -1):
        cand = mant + float(2 ** j)
        mant = jnp.where(enough(base * (1.0 + cand * (2.0 ** -23))), cand, mant)
    thr = base * (1.0 + mant * (2.0 ** -23))

    tri = tri_ref[...]

    def excl_cumsum(mask):
        out, off = [], jnp.zeros((N_EXPERTS, 1), F32)
        for c in range(n // LANES):
            mc = mask[:, c * LANES:(c + 1) * LANES]
            inc = jnp.dot(mc.astype(BF16), tri, preferred_element_type=F32)
            out.append(inc - mc + off)
            off = off + inc[:, LANES - 1:LANES]
        return jnp.concatenate(out, axis=1), off

    gt = jnp.where(aff > thr, 1.0, 0.0)
    eq = jnp.where(aff == thr, 1.0, 0.0)
    need = fcap - jnp.sum(gt, axis=1, keepdims=True)
    eq_rank, _ = excl_cumsum(eq)
    sel = jnp.maximum(gt, jnp.where(eq_rank < need, eq, 0.0))
    pos, _ = excl_cumsum(sel)
    aff_scr[...] = aff
    pos_scr[...] = jnp.where(sel > 0.0, pos, -1.0)

    tok = lax.broadcasted_iota(jnp.int32, (1, n), 1)
    tok_hi = (tok >> 6).astype(F32)
    tok_lo = (tok & 63).astype(F32)
    slot = lax.broadcasted_iota(jnp.int32, (cap, n), 0).astype(F32)
    zeros3 = jnp.zeros((3, n), F32)

    def per_expert(ex, carry):
        prow = pos_scr[pl.ds(ex, 1), :]
        arow = aff_scr[pl.ds(ex, 1), :]
        onehot = jnp.where(slot == prow, 1.0, 0.0).astype(BF16)
        a1 = arow.astype(BF16).astype(F32)
        r1 = arow - a1
        a2 = r1.astype(BF16).astype(F32)
        a3 = r1 - a2
        vals = jnp.concatenate([tok_hi, tok_lo, a1, a2, a3, zeros3], axis=0).astype(BF16)
        got = lax.dot_general(vals, onehot, (((1,), (1,)), ((), ())), preferred_element_type=F32)
        idx_ref[pl.ds(ex, 1), :] = (got[0:1] * 64.0 + got[1:2]).astype(jnp.int32)
        gate_ref[pl.ds(ex, 1), :] = got[2:3] + got[3:4] + got[4:5]
        return carry

    lax.fori_loop(0, N_EXPERTS, per_expert, 0)


def _route(logits_t, batch, cap):
    n = logits_t.shape[1] // batch
    tri = jnp.asarray(np.triu(np.ones((LANES, LANES))), dtype=BF16)
    return pl.pallas_call(
        functools.partial(_route_kernel, n=n, cap=cap),
        grid=(batch,),
        in_specs=[pl.BlockSpec((N_EXPERTS, n), lambda b: (0, b)),
                  pl.BlockSpec((LANES, LANES), lambda b: (0, 0))],
        out_specs=[pl.BlockSpec((None, N_EXPERTS, cap), lambda b: (b, 0, 0)),
                   pl.BlockSpec((None, N_EXPERTS, cap), lambda b: (b, 0, 0))],
        out_shape=[jax.ShapeDtypeStruct((batch, N_EXPERTS, cap), jnp.int32),
                   jax.ShapeDtypeStruct((batch, N_EXPERTS, cap), F32)],
        scratch_shapes=[pltpu.VMEM((N_EXPERTS, n), F32), pltpu.VMEM((N_EXPERTS, n), F32)],
        compiler_params=_params(("parallel",), 4 * cap * n * 4),
        name="route_topk",
    )(logits_t, tri)


def _gather_kernel(idx_ref, h_ref, xe_ref, rows_scr, *, cap):
    b, ex = pl.program_id(0), pl.program_id(1)
    base = (b * N_EXPERTS + ex) * cap
    unroll = SUBLANES

    def body(i, carry):
        for u in range(unroll):
            r = idx_ref[base + i * unroll + u]
            rows_scr[pl.ds(i * unroll + u, 1), :] = h_ref[pl.ds(r, 1), :]
        return carry

    lax.fori_loop(0, cap // unroll, body, 0)
    xe_ref[...] = rows_scr[...].astype(BF16)


def _gather(idx, h2, batch, cap):
    rows, d = h2.shape
    n = rows // batch
    grid_spec = pltpu.PrefetchScalarGridSpec(
        num_scalar_prefetch=1,
        grid=(batch, N_EXPERTS),
        in_specs=[pl.BlockSpec((n, d), lambda b, ex, idx_r: (b, 0))],
        out_specs=pl.BlockSpec((None, cap, d), lambda b, ex, idx_r: (ex, b, 0)),
        scratch_shapes=[pltpu.VMEM((cap, d), F32)],
    )
    return pl.pallas_call(
        functools.partial(_gather_kernel, cap=cap),
        grid_spec=grid_spec,
        out_shape=jax.ShapeDtypeStruct((N_EXPERTS, batch * cap, d), BF16),
        compiler_params=_params(("arbitrary", "arbitrary"), 2 * n * d * 4 + 3 * cap * d * 4),
        name="moe_gather",
    )(idx.reshape(-1), h2)


def _ffn_kernel(x_ref, wg_ref, wu_ref, wd_ref, y_ref, *, f_chunk):
    x = x_ref[...]
    ff = wg_ref.shape[1]
    y = None
    for c in range(ff // f_chunk):
        cs = slice(c * f_chunk, (c + 1) * f_chunk)
        g = jnp.dot(x, wg_ref[:, cs], preferred_element_type=F32)
        u = jnp.dot(x, wu_ref[:, cs], preferred_element_type=F32)
        a = (g / (1.0 + jnp.exp(-g)) * u).astype(BF16)
        part = jnp.dot(a, wd_ref[cs, :], preferred_element_type=F32)
        y = part if y is None else y + part
    y_ref[...] = y


def _expert_ffn(xe, layer, wg_bf, wu_bf, wd_bf, tm):
    n_exp, rows, d = xe.shape
    ff = wg_bf.shape[3]
    tm = min(tm, rows)
    return pl.pallas_call(
        functools.partial(_ffn_kernel, f_chunk=min(FFN_F_CHUNK, ff)),
        grid=(n_exp, rows // tm),
        in_specs=[
            pl.BlockSpec((None, tm, d), lambda ex, i: (ex, i, 0)),
            pl.BlockSpec((None, None, d, ff), lambda ex, i: (layer, ex, 0, 0)),
            pl.BlockSpec((None, None, d, ff), lambda ex, i: (layer, ex, 0, 0)),
            pl.BlockSpec((None, None, ff, d), lambda ex, i: (layer, ex, 0, 0)),
        ],
        out_specs=pl.BlockSpec((None, tm, d), lambda ex, i: (ex, i, 0)),
        out_shape=jax.ShapeDtypeStruct((n_exp, rows, d), F32),
        compiler_params=_params(("parallel", "arbitrary"), 2 * 3 * d * ff * 2 + 8 * tm * d * 4),
        name="expert_ffn",
    )(xe, wg_bf, wu_bf, wd_bf)


def _combine_kernel(idx_ref, gate_ref, y_ref, acc_ref, *, cap):
    b, ex = pl.program_id(0), pl.program_id(1)
    base = (b * N_EXPERTS + ex) * cap

    @pl.when(ex == 0)
    def _():
        acc_ref[...] = jnp.zeros_like(acc_ref)

    unroll = SUBLANES

    def body(i, carry):
        for u in range(unroll):
            j = i * unroll + u
            r = idx_ref[base + j]
            acc_ref[pl.ds(r, 1), :] = acc_ref[pl.ds(r, 1), :] + y_ref[pl.ds(j, 1), :] * gate_ref[base + j]
        return carry

    lax.fori_loop(0, cap // unroll, body, 0)


def _combine(idx, gate, y, batch, n, cap):
    d = y.shape[2]
    grid_spec = pltpu.PrefetchScalarGridSpec(
        num_scalar_prefetch=2,
        grid=(batch, N_EXPERTS),
        in_specs=[pl.BlockSpec((None, cap, d), lambda b, ex, idx_r, gate_r: (ex, b, 0))],
        out_specs=pl.BlockSpec((n, d), lambda b, ex, idx_r, gate_r: (b, 0)),
    )
    return pl.pallas_call(
        functools.partial(_combine_kernel, cap=cap),
        grid_spec=grid_spec,
        out_shape=jax.ShapeDtypeStruct((batch * n, d), F32),
        compiler_params=_params(("arbitrary", "arbitrary"), 2 * n * d * 4 + 2 * cap * d * 4),
        name="moe_combine",
    )(idx.reshape(-1), gate.reshape(-1), y)


def _resid_kernel(x_ref, m_ref, gate_ref, gf_ref, o_ref, *, final):
    x = x_ref[...] + gate_ref[...] * m_ref[...]
    o_ref[...] = _rms(x, gf_ref[...]) if final else x


def _residual(x2, moe, mod, mod_row, g_final, final, tm):
    rows, d = x2.shape
    row = lambda i: (i, 0)
    return pl.pallas_call(
        functools.partial(_resid_kernel, final=final),
        grid=(rows // tm,),
        in_specs=[pl.BlockSpec((tm, d), row), pl.BlockSpec((tm, d), row),
                  pl.BlockSpec((None, None, 1, d), lambda i: (mod_row(i), 5, 0, 0)),
                  pl.BlockSpec((1, d), lambda i: (0, 0))],
        out_specs=pl.BlockSpec((tm, d), row),
        out_shape=jax.ShapeDtypeStruct((rows, d), F32),
        compiler_params=_params(("parallel",), 6 * tm * d * 4),
        name="moe_residual_final" if final else "moe_residual",
    )(x2, moe, mod, g_final.reshape(1, d))


def _moe(h2, logits_t, batch, layer, wg_bf, wu_bf, wd_bf):
    n = h2.shape[0] // batch
    cap = max(1, CAPACITY_FACTOR * n // N_EXPERTS)
    idx, gate = _route(logits_t, batch, cap)
    xe = _gather(idx, h2, batch, cap)
    y = _expert_ffn(xe, layer, wg_bf, wu_bf, wd_bf, FFN_ROW_TILE)
    return _combine(idx, gate, y, batch, n, cap)


def _rope_tables(n):
    t = np.arange(n)
    n_freq = HEAD_DIM // 4
    inv = ROPE_THETA ** (-np.arange(n_freq, dtype=np.float64) / n_freq)
    ang_r = (t // GRID_W)[:, None] * inv[None, :]
    ang_c = (t % GRID_W)[:, None] * inv[None, :]
    ang = np.concatenate([ang_r, ang_r, ang_c, ang_c], axis=1)
    sign = np.concatenate([-np.ones(n_freq), np.ones(n_freq)] * 2)[None, :]
    reps = LANES // HEAD_DIM
    cos = np.tile(np.cos(ang), (1, reps))
    sin = np.tile(np.sin(ang) * sign, (1, reps))
    return jnp.asarray(cos, F32), jnp.asarray(sin, F32)


def kernel(x, c, ctx, c_ctx, w_ada, b_ada, g_mix, g_ffn, w_in, g_q, g_k, rel_bias, g_out_a, g_out_b, g_out_c, w_out, w_router, w_gate, w_up, w_down, g_final):
    batch, seq, d = x.shape
    ctx_len = ctx.shape[1]
    depth = w_ada.shape[0]
    b_w = g_out_b.shape[1]
    group_dim = b_w // B_GROUPS
    ctx_row = batch
    assert batch < MOD_ROWS and seq % ROW_TILE == 0

    cvec = jnp.zeros((MOD_ROWS, d), F32).at[:batch].set(c).at[ctx_row].set(c_ctx)
    mod_all = _ada(cvec, w_ada, b_ada).reshape(depth, MOD_ROWS, 6, 1, d)

    rope_tabs = _rope_tables(seq)
    blk = np.arange(LANES) // HEAD_DIM
    ones_hd = jnp.asarray(blk[:, None] == blk[None, :], dtype=BF16)
    cs = _channel_dft(b_w, group_dim)
    dft_lat = _dft_tables(seq, group_dim)
    dft_ctx = _dft_tables(ctx_len, group_dim)

    lat_tile = ROW_TILE
    ctx_tile = min(ROW_TILE, ctx_len)
    lat_row = lambda i: i // (seq // lat_tile)
    ctx_mod_row = lambda i: ctx_row
    group_a = A_Q_HEADS // A_KV_HEADS

    wg_bf, wu_bf, wd_bf = w_gate.astype(BF16), w_up.astype(BF16), w_down.astype(BF16)
    xl = x.reshape(batch * seq, d)
    xc = ctx.reshape(batch * ctx_len, d)
    for l in range(depth):
        last = l == depth - 1
        mod = mod_all[l]
        w_in_bf = w_in[l].astype(BF16)
        w_out_bf = w_out[l].astype(BF16)
        w_router_t = w_router[l].T
        gq_t = jnp.tile(g_q[l], LANES // HEAD_DIM).reshape(1, LANES)
        gk_t = jnp.tile(g_k[l], LANES // HEAD_DIM).reshape(1, LANES)

        qa, ka, va_t, yc, ys, qn, kn, vn = _inproj(xl, mod, lat_row, g_mix[l], w_in_bf, gq_t, gk_t, ones_hd, cs,
                                                   rope_tabs, seq, lat_tile)
        qa_c, ka_c, va_c_t, yc_c, ys_c, qn_c, kn_c, vn_c, vn_c_t = _inproj(
            xc, mod, ctx_mod_row, g_mix[l], w_in_bf, gq_t, gk_t, ones_hd, cs, None, ctx_len, ctx_tile)
        o_a_t = _attention(qa, ka, va_t, ka_c, va_c_t, batch, A_KV_HEADS, group_a, ATTN_Q_TILE)
        o_b = _fourier(dft_lat[0], dft_lat[1], yc, ys, batch, FOURIER_ROW_TILE)
        o_c = _natten(qn, kn, vn, kn_c, vn_c, rel_bias[l], batch)
        x_mid, h2, logits_t = _merge(o_a_t, o_b, o_c, False, xl, mod, lat_row, g_out_a[l], g_out_b[l],
                                     g_out_c[l], w_out_bf, g_ffn[l], w_router_t, lat_tile)
        moe = _moe(h2, logits_t, batch, l, wg_bf, wu_bf, wd_bf)
        xl = _residual(x_mid, moe, mod, lat_row, g_final, last, lat_tile)
        if not last:
            o_a_c_t = _attention(qa_c, ka_c, va_c_t, None, None, batch, A_KV_HEADS, group_a, ctx_len)
            o_b_c = _fourier(dft_ctx[0], dft_ctx[1], yc_c, ys_c, batch, ctx_len)
            o_c_c_t = _attention(qn_c, kn_c, vn_c_t, None, None, batch, C_HEADS, 1, ctx_len)
            xc_mid, h2_c, logits_c = _merge(o_a_c_t, o_b_c, o_c_c_t, True, xc, mod, ctx_mod_row, g_out_a[l],
                                            g_out_b[l], g_out_c[l], w_out_bf, g_ffn[l], w_router_t, ctx_tile)
            moe_c = _moe(h2_c, logits_c, batch, l, wg_bf, wu_bf, wd_bf)
            xc = _residual(xc_mid, moe_c, mod, ctx_mod_row, g_final, False, ctx_tile)
    return xl.reshape(batch, seq, d)
```

```python
import functools
import math

import numpy as np
import jax
import jax.numpy as jnp
from jax import lax
from jax.experimental import pallas as pl
from jax.experimental.pallas import tpu as pltpu

F32 = jnp.float32
BF16 = jnp.bfloat16
HIGHEST = lax.Precision.HIGHEST

GRID_W = 64
HEAD_DIM = 64
A_Q_HEADS = 8
A_KV_HEADS = 2
B_GROUPS = 4
C_HEADS = 4
NA_WIN_R = 8
NA_WIN_C = 16
ROPE_THETA = 10000.0
N_EXPERTS = 16
CAPACITY_FACTOR = 2
EPS = 1e-6
MOD_ROWS = 16

LANES = 128
SUBLANES = 8
VMEM_BYTES_V7X = 64 * 1024 * 1024

ROW_TILE = 512
ATTN_Q_TILE = 256
ATTN_KEY_CHUNK = 512
FOURIER_ROW_TILE = 512
NA_Q_ROWS = 4
FFN_ROW_TILE = 512
FFN_F_CHUNK = 512
NEG_BIG = -1e30


def _vmem_limit(nbytes):
    return int(min(max(2 * nbytes, 32 * 1024 * 1024), VMEM_BYTES_V7X - 8 * 1024 * 1024))


def _params(sem, nbytes):
    return pltpu.CompilerParams(dimension_semantics=sem, vmem_limit_bytes=_vmem_limit(nbytes))


def _rms(x, g):
    return x * lax.rsqrt(jnp.mean(x * x, axis=-1, keepdims=True) + EPS) * g


def _row_tiled_shape(rows, d):
    assert d % LANES == 0
    return (rows * (d // LANES), LANES)


def _store_row_tiled(ref, value):
    rows, d = value.shape
    chunks = d // LANES
    for c in range(chunks):
        ref[pl.ds(c, rows, stride=chunks), :] = value[:, c * LANES:(c + 1) * LANES]


def _load_row_tiled(ref, chunks):
    rows = ref.shape[0] // chunks
    return jnp.concatenate([ref[pl.ds(c, rows, stride=chunks), :] for c in range(chunks)], axis=1)


def _token_tile(i, chunks):
    return pl.ds(pl.multiple_of(i * chunks, chunks), chunks)


def _ada_kernel(c_ref, w_ref, b_ref, o_ref):
    c = c_ref[...]
    sc = c / (1.0 + jnp.exp(-c))
    o_ref[...] = jnp.dot(sc, w_ref[...], precision=HIGHEST, preferred_element_type=F32) + b_ref[...]


def _ada(cvec, w_ada, b_ada):
    depth, d, n6 = w_ada.shape
    tn = 1024
    return pl.pallas_call(
        _ada_kernel,
        grid=(depth, n6 // tn),
        in_specs=[
            pl.BlockSpec((MOD_ROWS, d), lambda l, j: (0, 0)),
            pl.BlockSpec((None, d, tn), lambda l, j: (l, 0, j)),
            pl.BlockSpec((None, 1, tn), lambda l, j: (l, 0, j)),
        ],
        out_specs=pl.BlockSpec((None, MOD_ROWS, tn), lambda l, j: (l, 0, j)),
        out_shape=jax.ShapeDtypeStruct((depth, MOD_ROWS, n6), F32),
        compiler_params=_params(("arbitrary", "arbitrary"), 2 * d * tn * 4),
        name="ada_mod",
    )(cvec, w_ada, b_ada.reshape(depth, 1, n6))


def _head_norm(t, ones, g):
    ssq = jnp.dot((t * t).astype(BF16), ones, preferred_element_type=F32)
    return t * lax.rsqrt(ssq * (1.0 / HEAD_DIM) + EPS) * g


def _rope(t, cos, sin):
    lane = lax.broadcasted_iota(jnp.int32, t.shape, 1)
    first = (lane & 31) < 16
    partner = jnp.where(first, pltpu.roll(t, LANES - 16, 1), pltpu.roll(t, 16, 1))
    return t * cos + partner * sin


def _inproj_kernel(*refs, rope, widths):
    a_w, kv_w, b_w, c_w = widths
    if rope:
        (x_ref, g_ref, shift_ref, scale_ref, w_ref, gq_ref, gk_ref, ones_ref, cs_ref, cos_ref, sin_ref,
         qa_ref, ka_ref, va_ref, yc_ref, ys_ref, qn_ref, kn_ref, vn_ref) = refs
        cos, sin = cos_ref[...], sin_ref[...]
    else:
        (x_ref, g_ref, shift_ref, scale_ref, w_ref, gq_ref, gk_ref, ones_ref, cs_ref,
         qa_ref, ka_ref, va_ref, yc_ref, ys_ref, qn_ref, kn_ref, vn_ref, vnt_ref) = refs
        cos = sin = None
    h = _rms(x_ref[...], g_ref[...]) * (1.0 + scale_ref[...]) + shift_ref[...]
    p = jnp.dot(h.astype(BF16), w_ref[...], preferred_element_type=F32)
    ones = ones_ref[...]
    q_scale = HEAD_DIM ** -0.5
    for j in range(a_w // LANES):
        t = _head_norm(p[:, j * LANES:(j + 1) * LANES], ones, gq_ref[...])
        if rope:
            t = _rope(t, cos, sin)
        qa_ref[:, j * LANES:(j + 1) * LANES] = (t * q_scale).astype(BF16)
    o = a_w
    t = _head_norm(p[:, o:o + kv_w], ones, gk_ref[...])
    if rope:
        t = _rope(t, cos, sin)
    ka_ref[...] = t.astype(BF16)
    o += kv_w
    va_ref[...] = jnp.transpose(p[:, o:o + kv_w]).astype(BF16)
    o += kv_w
    y = jnp.dot(p[:, o:o + b_w].astype(BF16), cs_ref[...], preferred_element_type=F32)
    yc_ref[...] = y[:, :b_w].astype(BF16)
    ys_ref[...] = y[:, b_w:].astype(BF16)
    o += b_w
    qn_ref[...] = (p[:, o:o + c_w] * q_scale).astype(BF16)
    o += c_w
    kn_ref[...] = p[:, o:o + c_w].astype(BF16)
    o += c_w
    vn_ref[...] = p[:, o:o + c_w].astype(BF16)
    if not rope:
        vnt_ref[...] = jnp.transpose(p[:, o:o + c_w]).astype(BF16)


def _inproj(x2, mod, mod_row, g_mix, w_in_bf, gq_t, gk_t, ones_hd, cs, rope_tabs, seq, tm):
    rows, d = x2.shape
    in_w = w_in_bf.shape[1]
    a_w = A_Q_HEADS * HEAD_DIM
    kv_w = A_KV_HEADS * HEAD_DIM
    b_w = cs.shape[0]
    c_w = C_HEADS * HEAD_DIM
    assert kv_w == LANES and in_w == a_w + 2 * kv_w + b_w + 3 * c_w
    rope = rope_tabs is not None
    steps_per_seq = seq // tm
    const = lambda i: (0, 0)
    row = lambda i: (i, 0)
    in_specs = [
        pl.BlockSpec((tm, d), row),
        pl.BlockSpec((1, d), const),
        pl.BlockSpec((None, None, 1, d), lambda i: (mod_row(i), 0, 0, 0)),
        pl.BlockSpec((None, None, 1, d), lambda i: (mod_row(i), 1, 0, 0)),
        pl.BlockSpec((d, in_w), const),
        pl.BlockSpec((1, LANES), const),
        pl.BlockSpec((1, LANES), const),
        pl.BlockSpec((LANES, LANES), const),
        pl.BlockSpec((b_w, 2 * b_w), const),
    ]
    args = [x2, g_mix.reshape(1, d), mod, mod, w_in_bf, gq_t, gk_t, ones_hd, cs]
    if rope:
        in_specs += [pl.BlockSpec((tm, LANES), lambda i: (i % steps_per_seq, 0))] * 2
        args += list(rope_tabs)
    out_w = [a_w, kv_w, kv_w, b_w, b_w, c_w, c_w, c_w]
    out_specs = [pl.BlockSpec((tm, w), row) for w in out_w]
    out_shape = [jax.ShapeDtypeStruct((rows, w), BF16) for w in out_w]
    out_specs[2] = pl.BlockSpec((kv_w, tm), lambda i: (0, i))
    out_shape[2] = jax.ShapeDtypeStruct((kv_w, rows), BF16)
    if not rope:
        out_specs.append(pl.BlockSpec((c_w, tm), lambda i: (0, i)))
        out_shape.append(jax.ShapeDtypeStruct((c_w, rows), BF16))
    return pl.pallas_call(
        functools.partial(_inproj_kernel, rope=rope, widths=(a_w, kv_w, b_w, c_w)),
        grid=(rows // tm,),
        in_specs=in_specs,
        out_specs=out_specs,
        out_shape=out_shape,
        compiler_params=_params(("parallel",), 2 * d * in_w * 2 + 2 * tm * d * 4 + 3 * tm * in_w * 4),
        name="inproj_rope" if rope else "inproj",
    )(*args)


def _place_head(q_ref, h, kv, kw):
    assert 2 * HEAD_DIM == LANES
    src_blk, src_off = divmod(h * HEAD_DIM, LANES)
    dst_blk, dst_off = divmod(kv * HEAD_DIM, LANES)
    blk = q_ref[:, src_blk * LANES:(src_blk + 1) * LANES].astype(F32)
    if src_off != dst_off:
        blk = pltpu.roll(blk, HEAD_DIM, 1)
    lane = lax.broadcasted_iota(jnp.int32, blk.shape, 1)
    keep = (lane >= dst_off) & (lane < dst_off + HEAD_DIM)
    blk = jnp.where(keep, blk, 0.0).astype(BF16)
    parts = [blk if j == dst_blk else jnp.zeros_like(blk) for j in range(kw // LANES)]
    return parts[0] if len(parts) == 1 else jnp.concatenate(parts, axis=1)


def _attn_kernel(*refs, n_kv, group, two):
    if two:
        q_ref, k1_ref, v1_ref, k2_ref, v2_ref, o_ref, s_scr = refs
        segments = ((k1_ref, v1_ref), (k2_ref, v2_ref))
    else:
        q_ref, k1_ref, v1_ref, o_ref, s_scr = refs
        segments = ((k1_ref, v1_ref),)
    kw = k1_ref.shape[1]
    tq = q_ref.shape[0]
    dn = (((1,), (1,)), ((), ()))
    sum_rows = 2 * SUBLANES
    n_heads = n_kv * group
    chunks, base = [], 0
    for seg, (k_ref, _) in enumerate(segments):
        n_seg = k_ref.shape[0]
        for c0 in range(0, n_seg, ATTN_KEY_CHUNK):
            chunks.append((seg, c0, min(ATTN_KEY_CHUNK, n_seg - c0), base + c0))
        base += n_seg

    def scores(h):
        slot = h % 2
        qp = _place_head(q_ref, h, h // group, kw)
        m = None
        for seg, c0, size, row in chunks:
            s = lax.dot_general(segments[seg][0][c0:c0 + size, :], qp, dn, preferred_element_type=F32)
            s_scr[slot, row:row + size, :] = s
            mc = jnp.max(s, axis=0, keepdims=True)
            m = mc if m is None else jnp.maximum(m, mc)
        return m

    def finish(h, m):
        slot, kv = h % 2, h // group
        vs = slice(kv * HEAD_DIM, (kv + 1) * HEAD_DIM)
        acc = jnp.zeros((HEAD_DIM + sum_rows, tq), F32)
        for seg, c0, size, row in chunks:
            va = jnp.concatenate([segments[seg][1][vs, c0:c0 + size], jnp.ones((sum_rows, size), BF16)], axis=0)
            p = jnp.exp((s_scr[slot, row:row + size, :] - m).astype(BF16))
            acc = acc + jnp.dot(va, p, preferred_element_type=F32)
        o = acc[:HEAD_DIM] * (1.0 / acc[HEAD_DIM:HEAD_DIM + 1])
        o_ref[h * HEAD_DIM:(h + 1) * HEAD_DIM, :] = o.astype(BF16)

    m_next = scores(0)
    for h in range(n_heads):
        m_cur = m_next
        if h + 1 < n_heads:
            m_next = scores(h + 1)
        finish(h, m_cur)


def _attention(q, k1, v1t, k2, v2t, batch, n_kv, group, tq):
    rows, qw = q.shape
    n_q = rows // batch
    n1 = k1.shape[0] // batch
    kw = k1.shape[1]
    two = k2 is not None
    steps = n_q // tq
    in_specs = [
        pl.BlockSpec((tq, qw), lambda b, i: (b * steps + i, 0)),
        pl.BlockSpec((n1, kw), lambda b, i: (b, 0)),
        pl.BlockSpec((kw, n1), lambda b, i: (0, b)),
    ]
    args = [q, k1, v1t]
    n_keys = n1
    if two:
        n2 = k2.shape[0] // batch
        in_specs += [pl.BlockSpec((n2, kw), lambda b, i: (b, 0)), pl.BlockSpec((kw, n2), lambda b, i: (0, b))]
        args += [k2, v2t]
        n_keys += n2
    return pl.pallas_call(
        functools.partial(_attn_kernel, n_kv=n_kv, group=group, two=two),
        grid=(batch, steps),
        in_specs=in_specs,
        out_specs=pl.BlockSpec((qw, tq), lambda b, i: (0, b * steps + i)),
        out_shape=jax.ShapeDtypeStruct((qw, rows), BF16),
        scratch_shapes=[pltpu.VMEM((2, n_keys, tq), F32)],
        compiler_params=_params(("parallel", "arbitrary"), 4 * n_keys * kw * 2 + 4 * tq * n_keys * 4),
        name="attn_two_seg" if two else "attn_one_seg",
    )(*args)


def _fourier_kernel(c_ref, s_ref, yc_ref, ys_ref, o_ref):
    o = jnp.dot(c_ref[...], yc_ref[...], preferred_element_type=F32)
    o = o + jnp.dot(s_ref[...], ys_ref[...], preferred_element_type=F32)
    o_ref[...] = o.astype(BF16)


def _fourier(c_tab, s_tab, yc, ys, batch, tm):
    n = c_tab.shape[0]
    w = yc.shape[1]
    steps = n // tm
    return pl.pallas_call(
        _fourier_kernel,
        grid=(steps, batch),
        in_specs=[
            pl.BlockSpec((tm, n), lambda i, b: (i, 0)),
            pl.BlockSpec((tm, n), lambda i, b: (i, 0)),
            pl.BlockSpec((n, w), lambda i, b: (b, 0)),
            pl.BlockSpec((n, w), lambda i, b: (b, 0)),
        ],
        out_specs=pl.BlockSpec((tm, w), lambda i, b: (b * steps + i, 0)),
        out_shape=jax.ShapeDtypeStruct((batch * n, w), BF16),
        compiler_params=_params(("arbitrary", "arbitrary"), 4 * tm * n * 2 + 4 * n * w * 2),
        name="fourier",
    )(c_tab, s_tab, yc, ys)


def _dft_tables(n, group_dim):
    scale = 1.0 / math.sqrt(n * group_dim)
    r = int(round(math.sqrt(n)))
    if r * r != n:
        k = jnp.arange(n, dtype=jnp.int32)
        ang = ((k[:, None] * k[None, :]) % n).astype(F32) * (2.0 * math.pi / n)
        return (jnp.cos(ang) * scale).astype(BF16), (-jnp.sin(ang) * scale).astype(BF16)
    k = jnp.arange(n, dtype=jnp.int32)[:, None]
    t = jnp.arange(r, dtype=jnp.int32)[None, :]
    ang_a = ((k * t) % r).astype(F32) * (2.0 * math.pi / r)
    ang_b = ((k * t) % n).astype(F32) * (2.0 * math.pi / n)
    ca, sa = jnp.cos(ang_a)[:, :, None], jnp.sin(ang_a)[:, :, None]
    cb, sb = jnp.cos(ang_b)[:, None, :], jnp.sin(ang_b)[:, None, :]
    c = (ca * cb - sa * sb) * scale
    s = (sa * cb + ca * sb) * (-scale)
    return c.reshape(n, n).astype(BF16), s.reshape(n, n).astype(BF16)


def _channel_dft(width, group_dim):
    j = np.arange(group_dim)
    ang = 2.0 * np.pi * ((j[:, None] * j[None, :]) % group_dim) / group_dim
    eye = np.eye(width // group_dim)
    cs = np.concatenate([np.kron(eye, np.cos(ang)), np.kron(eye, np.sin(ang))], axis=1)
    return jnp.asarray(cs, dtype=F32).astype(BF16)


def _na_kernel(ws_ref, var_ref, q_ref, k_ref, v_ref, kc_ref, vc_ref, bias_ref, o_ref, *, band):
    i = pl.program_id(1)
    start = pl.multiple_of(ws_ref[i] * GRID_W, GRID_W)
    var = var_ref[i]
    kw = k_ref[pl.ds(start, band), :]
    vw = v_ref[pl.ds(start, band), :]
    kc, vc = kc_ref[...], vc_ref[...]
    q = q_ref[...]
    dn = (((1,), (1,)), ((), ()))
    head_of_lane = lax.broadcasted_iota(jnp.int32, q.shape, 1) // HEAD_DIM
    out = jnp.zeros(q.shape, F32)
    for h in range(C_HEADS):
        mine = head_of_lane == h
        qh = jnp.where(mine, q, jnp.zeros_like(q))
        s1 = lax.dot_general(qh, kw, dn, preferred_element_type=F32) + bias_ref[var, h]
        s2 = lax.dot_general(qh, kc, dn, preferred_element_type=F32)
        m = jnp.maximum(jnp.max(s1, axis=-1, keepdims=True), jnp.max(s2, axis=-1, keepdims=True))
        p1 = jnp.exp((s1 - m).astype(BF16))
        p2 = jnp.exp((s2 - m).astype(BF16))
        l = jnp.sum(p1.astype(F32), axis=-1, keepdims=True) + jnp.sum(p2.astype(F32), axis=-1, keepdims=True)
        o = jnp.dot(p1, vw, preferred_element_type=F32) + jnp.dot(p2, vc, preferred_element_type=F32)
        out = jnp.where(mine, o * (1.0 / l), out)
    o_ref[...] = out.astype(BF16)


def _na_plan(seq):
    rows = seq // GRID_W
    wr = min(NA_WIN_R, rows)
    wc = min(NA_WIN_C, GRID_W)
    qr = min(NA_Q_ROWS, rows)
    band_rows = min(rows, 2 * ((qr + wr) // 2))
    nblk = rows // qr
    n_row_off, n_col_off = 2 * NA_WIN_R - 1, 2 * NA_WIN_C - 1
    c = np.arange(GRID_W)
    cs = np.clip(c - wc // 2, 0, GRID_W - wc)
    col_ok = (c[None, :] >= cs[:, None]) & (c[None, :] < cs[:, None] + wc)
    col_idx = np.where(col_ok, c[None, :] - c[:, None] + (NA_WIN_C - 1), n_col_off).astype(np.int32)
    ws_list, var_list, variants = [], [], []
    for blk in range(nblk):
        r = blk * qr + np.arange(qr)
        rs = np.clip(r - wr // 2, 0, rows - wr)
        ws = int(np.clip(rs[0], 0, rows - band_rows))
        kr = ws + np.arange(band_rows)
        row_ok = (kr[None, :] >= rs[:, None]) & (kr[None, :] < rs[:, None] + wr)
        assert row_ok.sum(axis=1).min() == wr
        idx = np.where(row_ok, kr[None, :] - r[:, None] + (NA_WIN_R - 1), n_row_off).astype(np.int32)
        for v, known in enumerate(variants):
            if np.array_equal(known, idx):
                break
        else:
            v = len(variants)
            variants.append(idx)
        ws_list.append(ws)
        var_list.append(v)
    return (np.asarray(ws_list, np.int32), np.asarray(var_list, np.int32),
            np.stack(variants), col_idx, qr, band_rows)


def _natten(qn, kn, vn, kc, vc, rel_bias_l, batch):
    rows, w = qn.shape
    seq = rows // batch
    ctx_len = kc.shape[0] // batch
    ws, var, row_idx, col_idx, qr, band_rows = _na_plan(seq)
    nblk = ws.shape[0]
    tq, band = qr * GRID_W, band_rows * GRID_W
    nvar = row_idx.shape[0]
    padded = jnp.pad(rel_bias_l, ((0, 0), (0, 1), (0, 1)), constant_values=NEG_BIG)
    tiles = padded[:, :, col_idx]
    bias = jnp.take(tiles, jnp.asarray(row_idx.reshape(-1)), axis=1)
    bias = bias.reshape(C_HEADS, nvar, qr, band_rows, GRID_W, GRID_W)
    bias = jnp.transpose(bias, (1, 0, 2, 4, 3, 5)).reshape(nvar, C_HEADS, tq, band)
    grid_spec = pltpu.PrefetchScalarGridSpec(
        num_scalar_prefetch=2,
        grid=(batch, nblk),
        in_specs=[
            pl.BlockSpec((tq, w), lambda b, i, ws_r, var_r: (b * nblk + i, 0)),
            pl.BlockSpec((seq, w), lambda b, i, ws_r, var_r: (b, 0)),
            pl.BlockSpec((seq, w), lambda b, i, ws_r, var_r: (b, 0)),
            pl.BlockSpec((ctx_len, w), lambda b, i, ws_r, var_r: (b, 0)),
            pl.BlockSpec((ctx_len, w), lambda b, i, ws_r, var_r: (b, 0)),
            pl.BlockSpec((nvar, C_HEADS, tq, band), lambda b, i, ws_r, var_r: (0, 0, 0, 0)),
        ],
        out_specs=pl.BlockSpec((tq, w), lambda b, i, ws_r, var_r: (b * nblk + i, 0)),
    )
    nbytes = 2 * nvar * C_HEADS * tq * band * 4 + 4 * seq * w * 2 + 6 * tq * (band + ctx_len) * 4
    return pl.pallas_call(
        functools.partial(_na_kernel, band=band),
        grid_spec=grid_spec,
        out_shape=jax.ShapeDtypeStruct((rows, w), BF16),
        compiler_params=_params(("parallel", "arbitrary"), nbytes),
        name="natten",
    )(jnp.asarray(ws), jnp.asarray(var), qn, kn, vn, kc, vc, bias)


def _merge_kernel(oa_ref, ob_ref, oc_ref, x_ref, gate_ref, shift_ref, scale_ref, ga_ref, gb_ref, gc_ref,
                  w_ref, gf_ref, wr_ref, xo_ref, h_ref, lg_ref, *, c_feature_major):
    def normed(o_ref, g_ref):
        return _rms(o_ref[...].astype(F32), g_ref[...]).astype(BF16)

    def normed_t(o_ref, g_ref):
        o = o_ref[...].astype(F32)
        return (o * lax.rsqrt(jnp.mean(o * o, axis=0, keepdims=True) + EPS) * g_ref[...]).astype(BF16)

    dn_t = (((0,), (0,)), ((), ()))
    a_w, b_w = oa_ref.shape[0], ob_ref.shape[1]
    y = lax.dot_general(normed_t(oa_ref, ga_ref), w_ref[0:a_w, :], dn_t, preferred_element_type=F32)
    y = y + jnp.dot(normed(ob_ref, gb_ref), w_ref[a_w:a_w + b_w, :], preferred_element_type=F32)
    if c_feature_major:
        y = y + lax.dot_general(normed_t(oc_ref, gc_ref), w_ref[a_w + b_w:, :], dn_t,
                                preferred_element_type=F32)
    else:
        y = y + jnp.dot(normed(oc_ref, gc_ref), w_ref[a_w + b_w:, :], preferred_element_type=F32)
    x = x_ref[...] + gate_ref[...] * y
    xo_ref[...] = x
    h = _rms(x, gf_ref[...]) * (1.0 + scale_ref[...]) + shift_ref[...]
    _store_row_tiled(h_ref, h)
    lg_ref[...] = lax.dot_general(wr_ref[...], h, (((1,), (1,)), ((), ())), precision=HIGHEST,
                                  preferred_element_type=F32)


def _merge(oa_t, ob, oc, c_feature_major, x2, mod, mod_row, g_a, g_b, g_c, w_out_bf, g_ffn, w_router_t, tm):
    rows, d = x2.shape
    a_w, b_w = oa_t.shape[0], ob.shape[1]
    row = lambda i: (i, 0)
    col = lambda i: (0, i)
    const = lambda i: (0, 0)
    mod_spec = lambda j: pl.BlockSpec((None, None, 1, d), lambda i: (mod_row(i), j, 0, 0))
    if c_feature_major:
        c_w = oc.shape[0]
        oc_spec, gc_spec, g_c2 = pl.BlockSpec((c_w, tm), col), pl.BlockSpec((c_w, 1), const), g_c.reshape(-1, 1)
    else:
        c_w = oc.shape[1]
        oc_spec, gc_spec, g_c2 = pl.BlockSpec((tm, c_w), row), pl.BlockSpec((1, c_w), const), g_c.reshape(1, -1)
    return pl.pallas_call(
        functools.partial(_merge_kernel, c_feature_major=c_feature_major),
        grid=(rows // tm,),
        in_specs=[
            pl.BlockSpec((a_w, tm), col), pl.BlockSpec((tm, b_w), row), oc_spec,
            pl.BlockSpec((tm, d), row),
            mod_spec(2), mod_spec(3), mod_spec(4),
            pl.BlockSpec((a_w, 1), const), pl.BlockSpec((1, b_w), const), gc_spec,
            pl.BlockSpec((a_w + b_w + c_w, d), const),
            pl.BlockSpec((1, d), const),
            pl.BlockSpec((N_EXPERTS, d), const),
        ],
        out_specs=[pl.BlockSpec((tm, d), row),
                   pl.BlockSpec(_row_tiled_shape(tm, d), row),
                   pl.BlockSpec((N_EXPERTS, tm), lambda i: (0, i))],
        out_shape=[jax.ShapeDtypeStruct((rows, d), F32), jax.ShapeDtypeStruct(_row_tiled_shape(rows, d), F32),
                   jax.ShapeDtypeStruct((N_EXPERTS, rows), F32)],
        compiler_params=_params(("parallel",), 2 * d * d * 2 + 8 * tm * d * 4),
        name="merge_out",
    )(oa_t, ob, oc, x2, mod, mod, mod, g_a.reshape(-1, 1), g_b.reshape(1, -1), g_c2,
      w_out_bf, g_ffn.reshape(1, d), w_router_t)


def _route_kernel(lg_ref, tri_ref, idx_ref, gate_ref, aff_scr, pos_scr, *, n, cap):
    lg = lg_ref[...]
    e = jnp.exp(lg - jnp.max(lg, axis=0, keepdims=True))
    aff = e / jnp.sum(e, axis=0, keepdims=True)
    fcap = float(cap)

    def enough(v):
        return jnp.sum(jnp.where(aff >= v, 1.0, 0.0), axis=1, keepdims=True) >= fcap

    p = jnp.full((N_EXPERTS, 1), 2.0, F32)
    for j in (64, 32, 16, 8, 4, 2, 1):
        cand = p * (2.0 ** -j)
        p = jnp.where(enough(cand), p, cand)
    base = p * 0.5
    mant = jnp.zeros((N_EXPERTS, 1), F32)
    for j in range(22, -1, -1):
        cand = mant + float(2 ** j)
        mant = jnp.where(enough(base * (1.0 + cand * (2.0 ** -23))), cand, mant)
    thr = base * (1.0 + mant * (2.0 ** -23))

    tri = tri_ref[...]

    def excl_cumsum(mask):
        out, off = [], jnp.zeros((N_EXPERTS, 1), F32)
        for c in range(n // LANES):
            mc = mask[:, c * LANES:(c + 1) * LANES]
            inc = jnp.dot(mc.astype(BF16), tri, preferred_element_type=F32)
            out.append(inc - mc + off)
            off = off + inc[:, LANES - 1:LANES]
        return jnp.concatenate(out, axis=1), off

    gt = jnp.where(aff > thr, 1.0, 0.0)
    eq = jnp.where(aff == thr, 1.0, 0.0)
    need = fcap - jnp.sum(gt, axis=1, keepdims=True)
    eq_rank, _ = excl_cumsum(eq)
    sel = jnp.maximum(gt, jnp.where(eq_rank < need, eq, 0.0))
    pos, _ = excl_cumsum(sel)
    aff_scr[...] = aff
    pos_scr[...] = jnp.where(sel > 0.0, pos, -1.0)

    tok = lax.broadcasted_iota(jnp.int32, (1, n), 1)
    tok_hi = (tok >> 6).astype(F32)
    tok_lo = (tok & 63).astype(F32)
    slot = lax.broadcasted_iota(jnp.int32, (cap, n), 0).astype(F32)
    zeros3 = jnp.zeros((3, n), F32)

    def per_expert(ex, carry):
        prow = pos_scr[pl.ds(ex, 1), :]
        arow = aff_scr[pl.ds(ex, 1), :]
        onehot = jnp.where(slot == prow, 1.0, 0.0).astype(BF16)
        a1 = arow.astype(BF16).astype(F32)
        r1 = arow - a1
        a2 = r1.astype(BF16).astype(F32)
        a3 = r1 - a2
        vals = jnp.concatenate([tok_hi, tok_lo, a1, a2, a3, zeros3], axis=0).astype(BF16)
        got = lax.dot_general(vals, onehot, (((1,), (1,)), ((), ())), preferred_element_type=F32)
        idx_ref[pl.ds(ex, 1), :] = (got[0:1] * 64.0 + got[1:2]).astype(jnp.int32)
        gate_ref[pl.ds(ex, 1), :] = got[2:3] + got[3:4] + got[4:5]
        return carry

    lax.fori_loop(0, N_EXPERTS, per_expert, 0)


def _route(logits_t, batch, cap):
    n = logits_t.shape[1] // batch
    tri = jnp.asarray(np.triu(np.ones((LANES, LANES))), dtype=BF16)
    return pl.pallas_call(
        functools.partial(_route_kernel, n=n, cap=cap),
        grid=(batch,),
        in_specs=[pl.BlockSpec((N_EXPERTS, n), lambda b: (0, b)),
                  pl.BlockSpec((LANES, LANES), lambda b: (0, 0))],
        out_specs=[pl.BlockSpec((None, N_EXPERTS, cap), lambda b: (b, 0, 0)),
                   pl.BlockSpec((None, N_EXPERTS, cap), lambda b: (b, 0, 0))],
        out_shape=[jax.ShapeDtypeStruct((batch, N_EXPERTS, cap), jnp.int32),
                   jax.ShapeDtypeStruct((batch, N_EXPERTS, cap), F32)],
        scratch_shapes=[pltpu.VMEM((N_EXPERTS, n), F32), pltpu.VMEM((N_EXPERTS, n), F32)],
        compiler_params=_params(("parallel",), 4 * cap * n * 4),
        name="route_topk",
    )(logits_t, tri)


def _gather_kernel(idx_ref, h_ref, xe_ref, rows_scr, *, cap, chunks):
    b, ex = pl.program_id(0), pl.program_id(1)
    base = (b * N_EXPERTS + ex) * cap
    unroll = 2 * SUBLANES

    def body(i, carry):
        rows = [h_ref[_token_tile(idx_ref[base + i * unroll + u], chunks), :] for u in range(unroll)]
        for u in range(unroll):
            rows_scr[_token_tile(i * unroll + u, chunks), :] = rows[u]
        return carry

    lax.fori_loop(0, cap // unroll, body, 0)
    xe_ref[...] = _load_row_tiled(rows_scr, chunks).astype(BF16)


def _gather(idx, h3, d, batch, cap):
    chunks = d // LANES
    n = h3.shape[0] // chunks // batch
    grid_spec = pltpu.PrefetchScalarGridSpec(
        num_scalar_prefetch=1,
        grid=(batch, N_EXPERTS),
        in_specs=[pl.BlockSpec(_row_tiled_shape(n, d), lambda b, ex, idx_r: (b, 0))],
        out_specs=pl.BlockSpec((None, cap, d), lambda b, ex, idx_r: (ex, b, 0)),
        scratch_shapes=[pltpu.VMEM(_row_tiled_shape(cap, d), F32)],
    )
    return pl.pallas_call(
        functools.partial(_gather_kernel, cap=cap, chunks=chunks),
        grid_spec=grid_spec,
        out_shape=jax.ShapeDtypeStruct((N_EXPERTS, batch * cap, d), BF16),
        compiler_params=_params(("arbitrary", "arbitrary"), 2 * n * d * 4 + 3 * cap * d * 4),
        name="moe_gather",
    )(idx.reshape(-1), h3)


def _ffn_kernel(x_ref, wg_ref, wu_ref, wd_ref, y_ref, *, f_chunk):
    x = x_ref[...]
    ff = wg_ref.shape[1]
    y = None
    for c in range(ff // f_chunk):
        cs = slice(c * f_chunk, (c + 1) * f_chunk)
        g = jnp.dot(x, wg_ref[:, cs], preferred_element_type=F32)
        u = jnp.dot(x, wu_ref[:, cs], preferred_element_type=F32)
        a = (g / (1.0 + jnp.exp(-g)) * u).astype(BF16)
        part = jnp.dot(a, wd_ref[cs, :], preferred_element_type=F32)
        y = part if y is None else y + part
    _store_row_tiled(y_ref, y)


def _expert_ffn(xe, layer, wg_bf, wu_bf, wd_bf, tm):
    n_exp, rows, d = xe.shape
    ff = wg_bf.shape[3]
    tm = min(tm, rows)
    return pl.pallas_call(
        functools.partial(_ffn_kernel, f_chunk=min(FFN_F_CHUNK, ff)),
        grid=(n_exp, rows // tm),
        in_specs=[
            pl.BlockSpec((None, tm, d), lambda ex, i: (ex, i, 0)),
            pl.BlockSpec((None, None, d, ff), lambda ex, i: (layer, ex, 0, 0)),
            pl.BlockSpec((None, None, d, ff), lambda ex, i: (layer, ex, 0, 0)),
            pl.BlockSpec((None, None, ff, d), lambda ex, i: (layer, ex, 0, 0)),
        ],
        out_specs=pl.BlockSpec((None,) + _row_tiled_shape(tm, d), lambda ex, i: (ex, i, 0)),
        out_shape=jax.ShapeDtypeStruct((n_exp,) + _row_tiled_shape(rows, d), F32),
        compiler_params=_params(("parallel", "arbitrary"), 2 * 3 * d * ff * 2 + 8 * tm * d * 4),
        name="expert_ffn",
    )(xe, wg_bf, wu_bf, wd_bf)


def _combine_kernel(idx_ref, gate_ref, y_ref, acc_ref, *, cap, chunks):
    b, ex = pl.program_id(0), pl.program_id(1)
    base = (b * N_EXPERTS + ex) * cap

    @pl.when(ex == 0)
    def _():
        acc_ref[...] = jnp.zeros_like(acc_ref)

    unroll = SUBLANES

    def body(i, carry):
        js = [i * unroll + u for u in range(unroll)]
        rs = [_token_tile(idx_ref[base + j], chunks) for j in js]
        new = [acc_ref[r, :] + y_ref[_token_tile(j, chunks), :] * gate_ref[base + j] for r, j in zip(rs, js)]
        for r, v in zip(rs, new):
            acc_ref[r, :] = v
        return carry

    lax.fori_loop(0, cap // unroll, body, 0)


def _combine(idx, gate, y3, d, batch, n, cap):
    chunks = d // LANES
    grid_spec = pltpu.PrefetchScalarGridSpec(
        num_scalar_prefetch=2,
        grid=(batch, N_EXPERTS),
        in_specs=[pl.BlockSpec((None,) + _row_tiled_shape(cap, d), lambda b, ex, idx_r, gate_r: (ex, b, 0))],
        out_specs=pl.BlockSpec(_row_tiled_shape(n, d), lambda b, ex, idx_r, gate_r: (b, 0)),
    )
    return pl.pallas_call(
        functools.partial(_combine_kernel, cap=cap, chunks=chunks),
        grid_spec=grid_spec,
        out_shape=jax.ShapeDtypeStruct(_row_tiled_shape(batch * n, d), F32),
        compiler_params=_params(("arbitrary", "arbitrary"), 2 * n * d * 4 + 2 * cap * d * 4),
        name="moe_combine",
    )(idx.reshape(-1), gate.reshape(-1), y3)


def _resid_kernel(x_ref, m_ref, gate_ref, gf_ref, o_ref, *, final):
    x = x_ref[...] + gate_ref[...] * _load_row_tiled(m_ref, x_ref.shape[1] // LANES)
    o_ref[...] = _rms(x, gf_ref[...]) if final else x


def _residual(x2, moe3, mod, mod_row, g_final, final, tm):
    rows, d = x2.shape
    row = lambda i: (i, 0)
    return pl.pallas_call(
        functools.partial(_resid_kernel, final=final),
        grid=(rows // tm,),
        in_specs=[pl.BlockSpec((tm, d), row), pl.BlockSpec(_row_tiled_shape(tm, d), row),
                  pl.BlockSpec((None, None, 1, d), lambda i: (mod_row(i), 5, 0, 0)),
                  pl.BlockSpec((1, d), lambda i: (0, 0))],
        out_specs=pl.BlockSpec((tm, d), row),
        out_shape=jax.ShapeDtypeStruct((rows, d), F32),
        compiler_params=_params(("parallel",), 6 * tm * d * 4),
        name="moe_residual_final" if final else "moe_residual",
    )(x2, moe3, mod, g_final.reshape(1, d))


def _moe(h3, logits_t, batch, layer, wg_bf, wu_bf, wd_bf):
    d = wg_bf.shape[2]
    n = logits_t.shape[1] // batch
    cap = max(1, CAPACITY_FACTOR * n // N_EXPERTS)
    idx, gate = _route(logits_t, batch, cap)
    xe = _gather(idx, h3, d, batch, cap)
    y3 = _expert_ffn(xe, layer, wg_bf, wu_bf, wd_bf, FFN_ROW_TILE)
    return _combine(idx, gate, y3, d, batch, n, cap)


def _rope_tables(n):
    t = np.arange(n)
    n_freq = HEAD_DIM // 4
    inv = ROPE_THETA ** (-np.arange(n_freq, dtype=np.float64) / n_freq)
    ang_r = (t // GRID_W)[:, None] * inv[None, :]
    ang_c = (t % GRID_W)[:, None] * inv[None, :]
    ang = np.concatenate([ang_r, ang_r, ang_c, ang_c], axis=1)
    sign = np.concatenate([-np.ones(n_freq), np.ones(n_freq)] * 2)[None, :]
    reps = LANES // HEAD_DIM
    cos = np.tile(np.cos(ang), (1, reps))
    sin = np.tile(np.sin(ang) * sign, (1, reps))
    return jnp.asarray(cos, F32), jnp.asarray(sin, F32)


def kernel(x, c, ctx, c_ctx, w_ada, b_ada, g_mix, g_ffn, w_in, g_q, g_k, rel_bias, g_out_a, g_out_b, g_out_c, w_out, w_router, w_gate, w_up, w_down, g_final):
    batch, seq, d = x.shape
    ctx_len = ctx.shape[1]
    depth = w_ada.shape[0]
    b_w = g_out_b.shape[1]
    group_dim = b_w // B_GROUPS
    ctx_row = batch
    assert batch < MOD_ROWS and seq % ROW_TILE == 0

    cvec = jnp.zeros((MOD_ROWS, d), F32).at[:batch].set(c).at[ctx_row].set(c_ctx)
    mod_all = _ada(cvec, w_ada, b_ada).reshape(depth, MOD_ROWS, 6, 1, d)

    rope_tabs = _rope_tables(seq)
    blk = np.arange(LANES) // HEAD_DIM
    ones_hd = jnp.asarray(blk[:, None] == blk[None, :], dtype=BF16)
    cs = _channel_dft(b_w, group_dim)
    dft_lat = _dft_tables(seq, group_dim)
    dft_ctx = _dft_tables(ctx_len, group_dim)

    lat_tile = ROW_TILE
    ctx_tile = min(ROW_TILE, ctx_len)
    lat_row = lambda i: i // (seq // lat_tile)
    ctx_mod_row = lambda i: ctx_row
    group_a = A_Q_HEADS // A_KV_HEADS

    wg_bf, wu_bf, wd_bf = w_gate.astype(BF16), w_up.astype(BF16), w_down.astype(BF16)
    xl = x.reshape(batch * seq, d)
    xc = ctx.reshape(batch * ctx_len, d)
    for l in range(depth):
        last = l == depth - 1
        mod = mod_all[l]
        w_in_bf = w_in[l].astype(BF16)
        w_out_bf = w_out[l].astype(BF16)
        w_router_t = w_router[l].T
        gq_t = jnp.tile(g_q[l], LANES // HEAD_DIM).reshape(1, LANES)
        gk_t = jnp.tile(g_k[l], LANES // HEAD_DIM).reshape(1, LANES)

        qa, ka, va_t, yc, ys, qn, kn, vn = _inproj(xl, mod, lat_row, g_mix[l], w_in_bf, gq_t, gk_t, ones_hd, cs,
                                                   rope_tabs, seq, lat_tile)
        qa_c, ka_c, va_c_t, yc_c, ys_c, qn_c, kn_c, vn_c, vn_c_t = _inproj(
            xc, mod, ctx_mod_row, g_mix[l], w_in_bf, gq_t, gk_t, ones_hd, cs, None, ctx_len, ctx_tile)
        o_a_t = _attention(qa, ka, va_t, ka_c, va_c_t, batch, A_KV_HEADS, group_a, ATTN_Q_TILE)
        o_b = _fourier(dft_lat[0], dft_lat[1], yc, ys, batch, FOURIER_ROW_TILE)
        o_c = _natten(qn, kn, vn, kn_c, vn_c, rel_bias[l], batch)
        x_mid, h2, logits_t = _merge(o_a_t, o_b, o_c, False, xl, mod, lat_row, g_out_a[l], g_out_b[l],
                                     g_out_c[l], w_out_bf, g_ffn[l], w_router_t, lat_tile)
        moe = _moe(h2, logits_t, batch, l, wg_bf, wu_bf, wd_bf)
        xl = _residual(x_mid, moe, mod, lat_row, g_final, last, lat_tile)
        if not last:
            o_a_c_t = _attention(qa_c, ka_c, va_c_t, None, None, batch, A_KV_HEADS, group_a, ctx_len)
            o_b_c = _fourier(dft_ctx[0], dft_ctx[1], yc_c, ys_c, batch, ctx_len)
            o_c_c_t = _attention(qn_c, kn_c, vn_c_t, None, None, batch, C_HEADS, 1, ctx_len)
            xc_mid, h2_c, logits_c = _merge(o_a_c_t, o_b_c, o_c_c_t, True, xc, mod, ctx_mod_row, g_out_a[l],
                                            g_out_b[l], g_out_c[l], w_out_bf, g_ffn[l], w_router_t, ctx_tile)
            moe_c = _moe(h2_c, logits_c, batch, l, wg_bf, wu_bf, wd_bf)
            xc = _residual(xc_mid, moe_c, mod, ctx_mod_row, g_final, False, ctx_tile)
    return xl.reshape(batch, seq, d)
```

```python
import functools
import math

import numpy as np
import jax
import jax.numpy as jnp
from jax import lax
from jax.experimental import pallas as pl
from jax.experimental.pallas import tpu as pltpu

F32 = jnp.float32
BF16 = jnp.bfloat16
HIGHEST = lax.Precision.HIGHEST

GRID_W = 64
HEAD_DIM = 64
A_Q_HEADS = 8
A_KV_HEADS = 2
B_GROUPS = 4
C_HEADS = 4
NA_WIN_R = 8
NA_WIN_C = 16
ROPE_THETA = 10000.0
N_EXPERTS = 16
CAPACITY_FACTOR = 2
EPS = 1e-6
MOD_ROWS = 16

LANES = 128
SUBLANES = 8
VMEM_BYTES_V7X = 64 * 1024 * 1024

ROW_TILE = 512
ATTN_Q_TILE = 256
ATTN_KEY_CHUNK = 512
ATTN_SCORE_SLOTS = 2
FOURIER_ROW_TILE = 512
NA_Q_ROWS = 4
FFN_ROW_TILE = 512
FFN_F_CHUNK = 512
FFN_F_SPLITS = 2
NEG_BIG = -1e30


def _vmem_limit(nbytes):
    return int(min(max(2 * nbytes, 32 * 1024 * 1024), VMEM_BYTES_V7X - 8 * 1024 * 1024))


def _params(sem, nbytes):
    return pltpu.CompilerParams(dimension_semantics=sem, vmem_limit_bytes=_vmem_limit(nbytes))


def _rms(x, g):
    return x * lax.rsqrt(jnp.mean(x * x, axis=-1, keepdims=True) + EPS) * g


def _row_tiled_shape(rows, d):
    assert d % LANES == 0
    return (rows * (d // LANES), LANES)


def _store_row_tiled(ref, value):
    rows, d = value.shape
    chunks = d // LANES
    for c in range(chunks):
        ref[pl.ds(c, rows, stride=chunks), :] = value[:, c * LANES:(c + 1) * LANES]


def _load_row_tiled(ref, chunks):
    rows = ref.shape[0] // chunks
    return jnp.concatenate([ref[pl.ds(c, rows, stride=chunks), :] for c in range(chunks)], axis=1)


def _token_tile(i, chunks):
    return pl.ds(pl.multiple_of(i * chunks, chunks), chunks)


def _ada_kernel(c_ref, w_ref, b_ref, o_ref):
    c = c_ref[...]
    sc = c / (1.0 + jnp.exp(-c))
    o_ref[...] = jnp.dot(sc, w_ref[...], precision=HIGHEST, preferred_element_type=F32) + b_ref[...]


def _ada(cvec, w_ada, b_ada):
    depth, d, n6 = w_ada.shape
    tn = 1024
    return pl.pallas_call(
        _ada_kernel,
        grid=(depth, n6 // tn),
        in_specs=[
            pl.BlockSpec((MOD_ROWS, d), lambda l, j: (0, 0)),
            pl.BlockSpec((None, d, tn), lambda l, j: (l, 0, j)),
            pl.BlockSpec((None, 1, tn), lambda l, j: (l, 0, j)),
        ],
        out_specs=pl.BlockSpec((None, MOD_ROWS, tn), lambda l, j: (l, 0, j)),
        out_shape=jax.ShapeDtypeStruct((depth, MOD_ROWS, n6), F32),
        compiler_params=_params(("arbitrary", "arbitrary"), 2 * d * tn * 4),
        name="ada_mod",
    )(cvec, w_ada, b_ada.reshape(depth, 1, n6))


def _head_norm(t, ones, g):
    ssq = jnp.dot((t * t).astype(BF16), ones, preferred_element_type=F32)
    return t * lax.rsqrt(ssq * (1.0 / HEAD_DIM) + EPS) * g


def _rope(t, cos, sin):
    lane = lax.broadcasted_iota(jnp.int32, t.shape, 1)
    first = (lane & 31) < 16
    partner = jnp.where(first, pltpu.roll(t, LANES - 16, 1), pltpu.roll(t, 16, 1))
    return t * cos + partner * sin


def _inproj_kernel(*refs, rope, widths):
    a_w, kv_w, b_w, c_w = widths
    if rope:
        (x_ref, g_ref, shift_ref, scale_ref, w_ref, gq_ref, gk_ref, ones_ref, cs_ref, cos_ref, sin_ref,
         qa_ref, ka_ref, va_ref, yc_ref, ys_ref, qn_ref, kn_ref, vn_ref) = refs
        cos, sin = cos_ref[...], sin_ref[...]
    else:
        (x_ref, g_ref, shift_ref, scale_ref, w_ref, gq_ref, gk_ref, ones_ref, cs_ref,
         qa_ref, ka_ref, va_ref, yc_ref, ys_ref, qn_ref, kn_ref, vn_ref, vnt_ref) = refs
        cos = sin = None
    h = _rms(x_ref[...], g_ref[...]) * (1.0 + scale_ref[...]) + shift_ref[...]
    p = jnp.dot(h.astype(BF16), w_ref[...], preferred_element_type=F32)
    ones = ones_ref[...]
    q_scale = HEAD_DIM ** -0.5
    for j in range(a_w // LANES):
        t = _head_norm(p[:, j * LANES:(j + 1) * LANES], ones, gq_ref[...])
        if rope:
            t = _rope(t, cos, sin)
        qa_ref[:, j * LANES:(j + 1) * LANES] = (t * q_scale).astype(BF16)
    o = a_w
    t = _head_norm(p[:, o:o + kv_w], ones, gk_ref[...])
    if rope:
        t = _rope(t, cos, sin)
    ka_ref[...] = t.astype(BF16)
    o += kv_w
    va_ref[...] = jnp.transpose(p[:, o:o + kv_w]).astype(BF16)
    o += kv_w
    y = jnp.dot(p[:, o:o + b_w].astype(BF16), cs_ref[...], preferred_element_type=F32)
    yc_ref[...] = y[:, :b_w].astype(BF16)
    ys_ref[...] = y[:, b_w:].astype(BF16)
    o += b_w
    qn_ref[...] = (p[:, o:o + c_w] * q_scale).astype(BF16)
    o += c_w
    kn_ref[...] = p[:, o:o + c_w].astype(BF16)
    o += c_w
    vn_ref[...] = p[:, o:o + c_w].astype(BF16)
    if not rope:
        vnt_ref[...] = jnp.transpose(p[:, o:o + c_w]).astype(BF16)


def _inproj(x2, mod, mod_row, g_mix, w_in_bf, gq_t, gk_t, ones_hd, cs, rope_tabs, seq, tm):
    rows, d = x2.shape
    in_w = w_in_bf.shape[1]
    a_w = A_Q_HEADS * HEAD_DIM
    kv_w = A_KV_HEADS * HEAD_DIM
    b_w = cs.shape[0]
    c_w = C_HEADS * HEAD_DIM
    assert kv_w == LANES and in_w == a_w + 2 * kv_w + b_w + 3 * c_w
    rope = rope_tabs is not None
    steps_per_seq = seq // tm
    const = lambda i: (0, 0)
    row = lambda i: (i, 0)
    in_specs = [
        pl.BlockSpec((tm, d), row),
        pl.BlockSpec((1, d), const),
        pl.BlockSpec((None, None, 1, d), lambda i: (mod_row(i), 0, 0, 0)),
        pl.BlockSpec((None, None, 1, d), lambda i: (mod_row(i), 1, 0, 0)),
        pl.BlockSpec((d, in_w), const),
        pl.BlockSpec((1, LANES), const),
        pl.BlockSpec((1, LANES), const),
        pl.BlockSpec((LANES, LANES), const),
        pl.BlockSpec((b_w, 2 * b_w), const),
    ]
    args = [x2, g_mix.reshape(1, d), mod, mod, w_in_bf, gq_t, gk_t, ones_hd, cs]
    if rope:
        in_specs += [pl.BlockSpec((tm, LANES), lambda i: (i % steps_per_seq, 0))] * 2
        args += list(rope_tabs)
    out_w = [a_w, kv_w, kv_w, b_w, b_w, c_w, c_w, c_w]
    out_specs = [pl.BlockSpec((tm, w), row) for w in out_w]
    out_shape = [jax.ShapeDtypeStruct((rows, w), BF16) for w in out_w]
    out_specs[2] = pl.BlockSpec((kv_w, tm), lambda i: (0, i))
    out_shape[2] = jax.ShapeDtypeStruct((kv_w, rows), BF16)
    if not rope:
        out_specs.append(pl.BlockSpec((c_w, tm), lambda i: (0, i)))
        out_shape.append(jax.ShapeDtypeStruct((c_w, rows), BF16))
    return pl.pallas_call(
        functools.partial(_inproj_kernel, rope=rope, widths=(a_w, kv_w, b_w, c_w)),
        grid=(rows // tm,),
        in_specs=in_specs,
        out_specs=out_specs,
        out_shape=out_shape,
        compiler_params=_params(("parallel",), 2 * d * in_w * 2 + 2 * tm * d * 4 + 3 * tm * in_w * 4),
        name="inproj_rope" if rope else "inproj",
    )(*args)


def _place_head(q_ref, h, kv, kw):
    assert 2 * HEAD_DIM == LANES
    src_blk, src_off = divmod(h * HEAD_DIM, LANES)
    dst_blk, dst_off = divmod(kv * HEAD_DIM, LANES)
    blk = q_ref[:, src_blk * LANES:(src_blk + 1) * LANES].astype(F32)
    if src_off != dst_off:
        blk = pltpu.roll(blk, HEAD_DIM, 1)
    lane = lax.broadcasted_iota(jnp.int32, blk.shape, 1)
    keep = (lane >= dst_off) & (lane < dst_off + HEAD_DIM)
    blk = jnp.transpose(jnp.where(keep, blk, 0.0)).astype(BF16)
    parts = [blk if j == dst_blk else jnp.zeros_like(blk) for j in range(kw // LANES)]
    return parts[0] if len(parts) == 1 else jnp.concatenate(parts, axis=0)


def _attn_kernel(*refs, n_kv, group, two):
    if two:
        q_ref, k1_ref, v1_ref, k2_ref, v2_ref, o_ref, s_scr = refs
        segments = ((k1_ref, v1_ref), (k2_ref, v2_ref))
    else:
        q_ref, k1_ref, v1_ref, o_ref, s_scr = refs
        segments = ((k1_ref, v1_ref),)
    kw = k1_ref.shape[1]
    tq = q_ref.shape[0]
    dn = (((1,), (1,)), ((), ()))
    sum_rows = 2 * SUBLANES
    n_heads = n_kv * group
    chunks, base = [], 0
    for seg, (k_ref, _) in enumerate(segments):
        n_seg = k_ref.shape[0]
        for c0 in range(0, n_seg, ATTN_KEY_CHUNK):
            chunks.append((seg, c0, min(ATTN_KEY_CHUNK, n_seg - c0), base + c0))
        base += n_seg

    def scores(h):
        slot = h % s_scr.shape[0]
        qp = _place_head(q_ref, h, h // group, kw)
        m = None
        for seg, c0, size, row in chunks:
            s = jnp.dot(segments[seg][0][c0:c0 + size, :], qp, preferred_element_type=F32)
            s_scr[slot, row:row + size, :] = s
            mc = jnp.max(s, axis=0, keepdims=True)
            m = mc if m is None else jnp.maximum(m, mc)
        return m

    def finish(h, m):
        slot, kv = h % s_scr.shape[0], h // group
        vs = slice(kv * HEAD_DIM, (kv + 1) * HEAD_DIM)
        acc = jnp.zeros((HEAD_DIM + sum_rows, tq), F32)
        for seg, c0, size, row in chunks:
            va = jnp.concatenate([segments[seg][1][vs, c0:c0 + size], jnp.ones((sum_rows, size), BF16)], axis=0)
            p = jnp.exp((s_scr[slot, row:row + size, :] - m).astype(BF16))
            acc = acc + jnp.dot(va, p, preferred_element_type=F32)
        o = acc[:HEAD_DIM] * (1.0 / acc[HEAD_DIM:HEAD_DIM + 1])
        o_ref[h * HEAD_DIM:(h + 1) * HEAD_DIM, :] = o.astype(BF16)

    n_slots = s_scr.shape[0]
    maxes = [scores(h) for h in range(min(n_slots - 1, n_heads))]
    for h in range(n_heads):
        if h + n_slots - 1 < n_heads:
            maxes.append(scores(h + n_slots - 1))
        finish(h, maxes[h])


def _attention(q, k1, v1t, k2, v2t, batch, n_kv, group, tq):
    rows, qw = q.shape
    n_q = rows // batch
    n1 = k1.shape[0] // batch
    kw = k1.shape[1]
    two = k2 is not None
    steps = n_q // tq
    in_specs = [
        pl.BlockSpec((tq, qw), lambda b, i: (b * steps + i, 0)),
        pl.BlockSpec((n1, kw), lambda b, i: (b, 0)),
        pl.BlockSpec((kw, n1), lambda b, i: (0, b)),
    ]
    args = [q, k1, v1t]
    n_keys = n1
    if two:
        n2 = k2.shape[0] // batch
        in_specs += [pl.BlockSpec((n2, kw), lambda b, i: (b, 0)), pl.BlockSpec((kw, n2), lambda b, i: (0, b))]
        args += [k2, v2t]
        n_keys += n2
    return pl.pallas_call(
        functools.partial(_attn_kernel, n_kv=n_kv, group=group, two=two),
        grid=(batch, steps),
        in_specs=in_specs,
        out_specs=pl.BlockSpec((qw, tq), lambda b, i: (0, b * steps + i)),
        out_shape=jax.ShapeDtypeStruct((qw, rows), BF16),
        scratch_shapes=[pltpu.VMEM((ATTN_SCORE_SLOTS, n_keys, tq), F32)],
        compiler_params=_params(("parallel", "arbitrary"), 4 * n_keys * kw * 2 + 4 * tq * n_keys * 4),
        name="attn_two_seg" if two else "attn_one_seg",
    )(*args)


def _fourier_kernel(c_ref, s_ref, yc_ref, ys_ref, o_ref):
    o = jnp.dot(c_ref[...], yc_ref[...], preferred_element_type=F32)
    o = o + jnp.dot(s_ref[...], ys_ref[...], preferred_element_type=F32)
    o_ref[...] = o.astype(BF16)


def _fourier(c_tab, s_tab, yc, ys, batch, tm):
    n = c_tab.shape[0]
    w = yc.shape[1]
    steps = n // tm
    return pl.pallas_call(
        _fourier_kernel,
        grid=(steps, batch),
        in_specs=[
            pl.BlockSpec((tm, n), lambda i, b: (i, 0)),
            pl.BlockSpec((tm, n), lambda i, b: (i, 0)),
            pl.BlockSpec((n, w), lambda i, b: (b, 0)),
            pl.BlockSpec((n, w), lambda i, b: (b, 0)),
        ],
        out_specs=pl.BlockSpec((tm, w), lambda i, b: (b * steps + i, 0)),
        out_shape=jax.ShapeDtypeStruct((batch * n, w), BF16),
        compiler_params=_params(("arbitrary", "arbitrary"), 4 * tm * n * 2 + 4 * n * w * 2),
        name="fourier",
    )(c_tab, s_tab, yc, ys)


def _dft_tables(n, group_dim):
    scale = 1.0 / math.sqrt(n * group_dim)
    r = int(round(math.sqrt(n)))
    if r * r != n:
        k = jnp.arange(n, dtype=jnp.int32)
        ang = ((k[:, None] * k[None, :]) % n).astype(F32) * (2.0 * math.pi / n)
        return (jnp.cos(ang) * scale).astype(BF16), (-jnp.sin(ang) * scale).astype(BF16)
    k = jnp.arange(n, dtype=jnp.int32)[:, None]
    t = jnp.arange(r, dtype=jnp.int32)[None, :]
    ang_a = ((k * t) % r).astype(F32) * (2.0 * math.pi / r)
    ang_b = ((k * t) % n).astype(F32) * (2.0 * math.pi / n)
    ca, sa = jnp.cos(ang_a)[:, :, None], jnp.sin(ang_a)[:, :, None]
    cb, sb = jnp.cos(ang_b)[:, None, :], jnp.sin(ang_b)[:, None, :]
    c = (ca * cb - sa * sb) * scale
    s = (sa * cb + ca * sb) * (-scale)
    return c.reshape(n, n).astype(BF16), s.reshape(n, n).astype(BF16)


def _channel_dft(width, group_dim):
    j = np.arange(group_dim)
    ang = 2.0 * np.pi * ((j[:, None] * j[None, :]) % group_dim) / group_dim
    eye = np.eye(width // group_dim)
    cs = np.concatenate([np.kron(eye, np.cos(ang)), np.kron(eye, np.sin(ang))], axis=1)
    return jnp.asarray(cs, dtype=F32).astype(BF16)


def _na_kernel(ws_ref, var_ref, q_ref, k_ref, v_ref, kc_ref, vc_ref, bias_ref, o_ref, *, band):
    i = pl.program_id(1)
    start = pl.multiple_of(ws_ref[i] * GRID_W, GRID_W)
    var = var_ref[i]
    kw = k_ref[pl.ds(start, band), :]
    vw = v_ref[pl.ds(start, band), :]
    kc, vc = kc_ref[...], vc_ref[...]
    q = q_ref[...]
    dn = (((1,), (1,)), ((), ()))
    head_of_lane = lax.broadcasted_iota(jnp.int32, q.shape, 1) // HEAD_DIM
    out = jnp.zeros(q.shape, F32)
    for h in range(C_HEADS):
        mine = head_of_lane == h
        qh = jnp.where(mine, q, jnp.zeros_like(q))
        s1 = lax.dot_general(qh, kw, dn, preferred_element_type=F32) + bias_ref[var, h]
        s2 = lax.dot_general(qh, kc, dn, preferred_element_type=F32)
        m = jnp.maximum(jnp.max(s1, axis=-1, keepdims=True), jnp.max(s2, axis=-1, keepdims=True))
        p1 = jnp.exp((s1 - m).astype(BF16))
        p2 = jnp.exp((s2 - m).astype(BF16))
        l = jnp.sum(p1.astype(F32), axis=-1, keepdims=True) + jnp.sum(p2.astype(F32), axis=-1, keepdims=True)
        o = jnp.dot(p1, vw, preferred_element_type=F32) + jnp.dot(p2, vc, preferred_element_type=F32)
        out = jnp.where(mine, o * (1.0 / l), out)
    o_ref[...] = out.astype(BF16)


def _na_plan(seq):
    rows = seq // GRID_W
    wr = min(NA_WIN_R, rows)
    wc = min(NA_WIN_C, GRID_W)
    qr = min(NA_Q_ROWS, rows)
    band_rows = min(rows, 2 * ((qr + wr) // 2))
    nblk = rows // qr
    n_row_off, n_col_off = 2 * NA_WIN_R - 1, 2 * NA_WIN_C - 1
    c = np.arange(GRID_W)
    cs = np.clip(c - wc // 2, 0, GRID_W - wc)
    col_ok = (c[None, :] >= cs[:, None]) & (c[None, :] < cs[:, None] + wc)
    col_idx = np.where(col_ok, c[None, :] - c[:, None] + (NA_WIN_C - 1), n_col_off).astype(np.int32)
    ws_list, var_list, variants = [], [], []
    for blk in range(nblk):
        r = blk * qr + np.arange(qr)
        rs = np.clip(r - wr // 2, 0, rows - wr)
        ws = int(np.clip(rs[0], 0, rows - band_rows))
        kr = ws + np.arange(band_rows)
        row_ok = (kr[None, :] >= rs[:, None]) & (kr[None, :] < rs[:, None] + wr)
        assert row_ok.sum(axis=1).min() == wr
        idx = np.where(row_ok, kr[None, :] - r[:, None] + (NA_WIN_R - 1), n_row_off).astype(np.int32)
        for v, known in enumerate(variants):
            if np.array_equal(known, idx):
                break
        else:
            v = len(variants)
            variants.append(idx)
        ws_list.append(ws)
        var_list.append(v)
    return (np.asarray(ws_list, np.int32), np.asarray(var_list, np.int32),
            np.stack(variants), col_idx, qr, band_rows)


def _natten(qn, kn, vn, kc, vc, rel_bias_l, batch):
    rows, w = qn.shape
    seq = rows // batch
    ctx_len = kc.shape[0] // batch
    ws, var, row_idx, col_idx, qr, band_rows = _na_plan(seq)
    nblk = ws.shape[0]
    tq, band = qr * GRID_W, band_rows * GRID_W
    nvar = row_idx.shape[0]
    padded = jnp.pad(rel_bias_l, ((0, 0), (0, 1), (0, 1)), constant_values=NEG_BIG)
    tiles = padded[:, :, col_idx]
    bias = jnp.take(tiles, jnp.asarray(row_idx.reshape(-1)), axis=1)
    bias = bias.reshape(C_HEADS, nvar, qr, band_rows, GRID_W, GRID_W)
    bias = jnp.transpose(bias, (1, 0, 2, 4, 3, 5)).reshape(nvar, C_HEADS, tq, band)
    grid_spec = pltpu.PrefetchScalarGridSpec(
        num_scalar_prefetch=2,
        grid=(batch, nblk),
        in_specs=[
            pl.BlockSpec((tq, w), lambda b, i, ws_r, var_r: (b * nblk + i, 0)),
            pl.BlockSpec((seq, w), lambda b, i, ws_r, var_r: (b, 0)),
            pl.BlockSpec((seq, w), lambda b, i, ws_r, var_r: (b, 0)),
            pl.BlockSpec((ctx_len, w), lambda b, i, ws_r, var_r: (b, 0)),
            pl.BlockSpec((ctx_len, w), lambda b, i, ws_r, var_r: (b, 0)),
            pl.BlockSpec((nvar, C_HEADS, tq, band), lambda b, i, ws_r, var_r: (0, 0, 0, 0)),
        ],
        out_specs=pl.BlockSpec((tq, w), lambda b, i, ws_r, var_r: (b * nblk + i, 0)),
    )
    nbytes = 2 * nvar * C_HEADS * tq * band * 4 + 4 * seq * w * 2 + 6 * tq * (band + ctx_len) * 4
    return pl.pallas_call(
        functools.partial(_na_kernel, band=band),
        grid_spec=grid_spec,
        out_shape=jax.ShapeDtypeStruct((rows, w), BF16),
        compiler_params=_params(("parallel", "arbitrary"), nbytes),
        name="natten",
    )(jnp.asarray(ws), jnp.asarray(var), qn, kn, vn, kc, vc, bias)


def _merge_kernel(oa_ref, ob_ref, oc_ref, x_ref, gate_ref, shift_ref, scale_ref, ga_ref, gb_ref, gc_ref,
                  w_ref, gf_ref, wa_ref, wb_ref, xo_ref, h_ref, lg_ref, *, c_feature_major):
    def normed(o_ref, g_ref):
        return _rms(o_ref[...].astype(F32), g_ref[...]).astype(BF16)

    def normed_t(o_ref, g_ref):
        o = o_ref[...].astype(F32)
        return (o * lax.rsqrt(jnp.mean(o * o, axis=0, keepdims=True) + EPS) * g_ref[...]).astype(BF16)

    dn_t = (((0,), (0,)), ((), ()))
    a_w, b_w = oa_ref.shape[0], ob_ref.shape[1]
    y = lax.dot_general(normed_t(oa_ref, ga_ref), w_ref[0:a_w, :], dn_t, preferred_element_type=F32)
    y = y + jnp.dot(normed(ob_ref, gb_ref), w_ref[a_w:a_w + b_w, :], preferred_element_type=F32)
    if c_feature_major:
        y = y + lax.dot_general(normed_t(oc_ref, gc_ref), w_ref[a_w + b_w:, :], dn_t,
                                preferred_element_type=F32)
    else:
        y = y + jnp.dot(normed(oc_ref, gc_ref), w_ref[a_w + b_w:, :], preferred_element_type=F32)
    x = x_ref[...] + gate_ref[...] * y
    xo_ref[...] = x
    h = _rms(x, gf_ref[...]) * (1.0 + scale_ref[...]) + shift_ref[...]
    _store_row_tiled(h_ref, h)
    h_hi = h.astype(BF16)
    h_lo = (h - h_hi.astype(F32)).astype(BF16)
    t = jnp.dot(h_hi, wa_ref[...], preferred_element_type=F32) + jnp.dot(h_lo, wb_ref[...], preferred_element_type=F32)
    t = jnp.transpose(t)
    lg_ref[...] = t[0:N_EXPERTS] + t[N_EXPERTS:2 * N_EXPERTS]


def _router_split(w_router_l):
    d, n_exp = w_router_l.shape
    w_hi = w_router_l.astype(BF16)
    w_lo = (w_router_l - w_hi.astype(F32)).astype(BF16)
    wa = jnp.concatenate([w_hi, w_lo, jnp.zeros((d, LANES - 2 * n_exp), BF16)], axis=1)
    wb = jnp.concatenate([w_hi, jnp.zeros((d, LANES - n_exp), BF16)], axis=1)
    return wa, wb


def _merge(oa_t, ob, oc, c_feature_major, x2, mod, mod_row, g_a, g_b, g_c, w_out_bf, g_ffn, w_router_ab, tm):
    rows, d = x2.shape
    a_w, b_w = oa_t.shape[0], ob.shape[1]
    row = lambda i: (i, 0)
    col = lambda i: (0, i)
    const = lambda i: (0, 0)
    mod_spec = lambda j: pl.BlockSpec((None, None, 1, d), lambda i: (mod_row(i), j, 0, 0))
    if c_feature_major:
        c_w = oc.shape[0]
        oc_spec, gc_spec, g_c2 = pl.BlockSpec((c_w, tm), col), pl.BlockSpec((c_w, 1), const), g_c.reshape(-1, 1)
    else:
        c_w = oc.shape[1]
        oc_spec, gc_spec, g_c2 = pl.BlockSpec((tm, c_w), row), pl.BlockSpec((1, c_w), const), g_c.reshape(1, -1)
    return pl.pallas_call(
        functools.partial(_merge_kernel, c_feature_major=c_feature_major),
        grid=(rows // tm,),
        in_specs=[
            pl.BlockSpec((a_w, tm), col), pl.BlockSpec((tm, b_w), row), oc_spec,
            pl.BlockSpec((tm, d), row),
            mod_spec(2), mod_spec(3), mod_spec(4),
            pl.BlockSpec((a_w, 1), const), pl.BlockSpec((1, b_w), const), gc_spec,
            pl.BlockSpec((a_w + b_w + c_w, d), const),
            pl.BlockSpec((1, d), const),
            pl.BlockSpec((d, LANES), const),
            pl.BlockSpec((d, LANES), const),
        ],
        out_specs=[pl.BlockSpec((tm, d), row),
                   pl.BlockSpec(_row_tiled_shape(tm, d), row),
                   pl.BlockSpec((N_EXPERTS, tm), lambda i: (0, i))],
        out_shape=[jax.ShapeDtypeStruct((rows, d), F32), jax.ShapeDtypeStruct(_row_tiled_shape(rows, d), F32),
                   jax.ShapeDtypeStruct((N_EXPERTS, rows), F32)],
        compiler_params=_params(("parallel",), 2 * d * d * 2 + 8 * tm * d * 4),
        name="merge_out",
    )(oa_t, ob, oc, x2, mod, mod, mod, g_a.reshape(-1, 1), g_b.reshape(1, -1), g_c2,
      w_out_bf, g_ffn.reshape(1, d), *w_router_ab)


def _route_kernel(lg_ref, tri_ref, idx_ref, gate_ref, *, n, cap):
    lg = lg_ref[...]
    e = jnp.exp(lg - jnp.max(lg, axis=0, keepdims=True))
    aff = e / jnp.sum(e, axis=0, keepdims=True)
    fcap = float(cap)

    def enough(v):
        return jnp.sum(jnp.where(aff >= v, 1.0, 0.0), axis=1, keepdims=True) >= fcap

    p = jnp.full((N_EXPERTS, 1), 2.0, F32)
    for j in (64, 32, 16, 8, 4, 2, 1):
        cand = p * (2.0 ** -j)
        p = jnp.where(enough(cand), p, cand)
    base = p * 0.5
    mant = jnp.zeros((N_EXPERTS, 1), F32)
    for j in range(22, -1, -1):
        cand = mant + float(2 ** j)
        mant = jnp.where(enough(base * (1.0 + cand * (2.0 ** -23))), cand, mant)
    thr = base * (1.0 + mant * (2.0 ** -23))

    tri = tri_ref[...]

    def excl_cumsum(mask):
        out, off = [], jnp.zeros((N_EXPERTS, 1), F32)
        for c in range(n // LANES):
            mc = mask[:, c * LANES:(c + 1) * LANES]
            inc = jnp.dot(mc.astype(BF16), tri, preferred_element_type=F32)
            out.append(inc - mc + off)
            off = off + inc[:, LANES - 1:LANES]
        return jnp.concatenate(out, axis=1), off

    gt = jnp.where(aff > thr, 1.0, 0.0)
    eq = jnp.where(aff == thr, 1.0, 0.0)
    need = fcap - jnp.sum(gt, axis=1, keepdims=True)
    eq_rank, _ = excl_cumsum(eq)
    sel = jnp.maximum(gt, jnp.where(eq_rank < need, eq, 0.0))
    pos, _ = excl_cumsum(sel)

    chosen = sel > 0.0
    tok = lax.broadcasted_iota(jnp.int32, (N_EXPERTS, n), 1)
    dist = jnp.where(chosen, tok - pos.astype(jnp.int32), 0)
    tok1 = jnp.where(chosen, tok + 1, 0)
    gate = jnp.where(chosen, aff, 0.0)
    for bit in range((n - 1).bit_length()):
        left = n - (1 << bit)
        dist_in, tok_in, gate_in = pltpu.roll(dist, left, 1), pltpu.roll(tok1, left, 1), pltpu.roll(gate, left, 1)
        arrives = ((dist_in >> bit) & 1) == 1
        stays = ((dist >> bit) & 1) == 0
        tok1 = jnp.where(arrives, tok_in, jnp.where(stays, tok1, 0))
        gate = jnp.where(arrives, gate_in, jnp.where(stays, gate, 0.0))
        dist = jnp.where(arrives, dist_in, jnp.where(stays, dist, 0))
    idx_ref[...] = tok1[:, :cap] - 1
    gate_ref[...] = gate[:, :cap]


def _route(logits_t, batch, cap):
    n = logits_t.shape[1] // batch
    assert n & (n - 1) == 0 and n % LANES == 0
    tri = jnp.asarray(np.triu(np.ones((LANES, LANES))), dtype=BF16)
    return pl.pallas_call(
        functools.partial(_route_kernel, n=n, cap=cap),
        grid=(batch,),
        in_specs=[pl.BlockSpec((N_EXPERTS, n), lambda b: (0, b)),
                  pl.BlockSpec((LANES, LANES), lambda b: (0, 0))],
        out_specs=[pl.BlockSpec((None, N_EXPERTS, cap), lambda b: (b, 0, 0)),
                   pl.BlockSpec((None, N_EXPERTS, cap), lambda b: (b, 0, 0))],
        out_shape=[jax.ShapeDtypeStruct((batch, N_EXPERTS, cap), jnp.int32),
                   jax.ShapeDtypeStruct((batch, N_EXPERTS, cap), F32)],
        compiler_params=_params(("parallel",), 64 * N_EXPERTS * n * 4),
        name="route_topk",
    )(logits_t, tri)


def _gather_kernel(idx_ref, h_ref, xe_ref, rows_scr, *, cap, chunks):
    b, ex = pl.program_id(0), pl.program_id(1)
    base = (b * N_EXPERTS + ex) * cap
    unroll = 2 * SUBLANES

    def body(i, carry):
        rows = [h_ref[_token_tile(idx_ref[base + i * unroll + u], chunks), :] for u in range(unroll)]
        for u in range(unroll):
            rows_scr[_token_tile(i * unroll + u, chunks), :] = rows[u]
        return carry

    lax.fori_loop(0, cap // unroll, body, 0)
    xe_ref[...] = _load_row_tiled(rows_scr, chunks).astype(BF16)


def _gather(idx, h3, d, batch, cap):
    chunks = d // LANES
    n = h3.shape[0] // chunks // batch
    grid_spec = pltpu.PrefetchScalarGridSpec(
        num_scalar_prefetch=1,
        grid=(batch, N_EXPERTS),
        in_specs=[pl.BlockSpec(_row_tiled_shape(n, d), lambda b, ex, idx_r: (b, 0))],
        out_specs=pl.BlockSpec((None, cap, d), lambda b, ex, idx_r: (ex, b, 0)),
        scratch_shapes=[pltpu.VMEM(_row_tiled_shape(cap, d), F32)],
    )
    return pl.pallas_call(
        functools.partial(_gather_kernel, cap=cap, chunks=chunks),
        grid_spec=grid_spec,
        out_shape=jax.ShapeDtypeStruct((N_EXPERTS, batch * cap, d), BF16),
        compiler_params=_params(("arbitrary", "arbitrary"), 2 * n * d * 4 + 3 * cap * d * 4),
        name="moe_gather",
    )(idx.reshape(-1), h3)


def _ffn_kernel(x_ref, wg_ref, wu_ref, wd_ref, y_ref, wg_bf, wu_bf, wd_bf, *, f_chunk):
    @pl.when(pl.program_id(2) == 0)
    def _():
        wg_bf[...] = wg_ref[...].astype(BF16)
        wu_bf[...] = wu_ref[...].astype(BF16)
        wd_bf[...] = wd_ref[...].astype(BF16)

    x = x_ref[...]
    y = None
    for c in range(wg_bf.shape[1] // f_chunk):
        cs = slice(c * f_chunk, (c + 1) * f_chunk)
        g = jnp.dot(x, wg_bf[:, cs], preferred_element_type=F32)
        u = jnp.dot(x, wu_bf[:, cs], preferred_element_type=F32)
        a = (g / (1.0 + jnp.exp(-g)) * u).astype(BF16)
        part = jnp.dot(a, wd_bf[cs, :], preferred_element_type=F32)
        y = part if y is None else y + part
    _store_row_tiled(y_ref, y)


def _expert_ffn(xe, layer, w_gate, w_up, w_down, tm):
    n_exp, rows, d = xe.shape
    ff = w_gate.shape[3]
    fh = ff // FFN_F_SPLITS
    tm = min(tm, rows)
    return pl.pallas_call(
        functools.partial(_ffn_kernel, f_chunk=min(FFN_F_CHUNK, fh)),
        grid=(n_exp, FFN_F_SPLITS, rows // tm),
        in_specs=[
            pl.BlockSpec((None, tm, d), lambda ex, s, i: (ex, i, 0)),
            pl.BlockSpec((None, None, d, fh), lambda ex, s, i: (layer, ex, 0, s)),
            pl.BlockSpec((None, None, d, fh), lambda ex, s, i: (layer, ex, 0, s)),
            pl.BlockSpec((None, None, fh, d), lambda ex, s, i: (layer, ex, s, 0)),
        ],
        out_specs=pl.BlockSpec((None, None) + _row_tiled_shape(tm, d), lambda ex, s, i: (s, ex, i, 0)),
        out_shape=jax.ShapeDtypeStruct((FFN_F_SPLITS, n_exp) + _row_tiled_shape(rows, d), F32),
        scratch_shapes=[pltpu.VMEM((d, fh), BF16), pltpu.VMEM((d, fh), BF16), pltpu.VMEM((fh, d), BF16)],
        compiler_params=_params(("arbitrary", "arbitrary", "arbitrary"),
                                2 * 3 * d * fh * 4 + 3 * d * fh * 2 + 8 * tm * d * 4),
        name="expert_ffn",
    )(xe, w_gate, w_up, w_down)


def _combine_kernel(idx_ref, gate_ref, y_ref, acc_ref, *, cap, chunks):
    b, ex = pl.program_id(0), pl.program_id(1)
    base = (b * N_EXPERTS + ex) * cap

    @pl.when(ex == 0)
    def _():
        acc_ref[...] = jnp.zeros_like(acc_ref)

    unroll = SUBLANES

    def body(i, carry):
        js = [i * unroll + u for u in range(unroll)]
        rs = [_token_tile(idx_ref[base + j], chunks) for j in js]

        def expert_row(j):
            t = _token_tile(j, chunks)
            row = y_ref[0, t, :]
            for s in range(1, y_ref.shape[0]):
                row = row + y_ref[s, t, :]
            return row

        new = [acc_ref[r, :] + expert_row(j) * gate_ref[base + j] for r, j in zip(rs, js)]
        for r, v in zip(rs, new):
            acc_ref[r, :] = v
        return carry

    lax.fori_loop(0, cap // unroll, body, 0)


def _combine(idx, gate, y3, d, batch, n, cap):
    chunks = d // LANES
    n_split = y3.shape[0]
    grid_spec = pltpu.PrefetchScalarGridSpec(
        num_scalar_prefetch=2,
        grid=(batch, N_EXPERTS),
        in_specs=[pl.BlockSpec((n_split, None) + _row_tiled_shape(cap, d),
                               lambda b, ex, idx_r, gate_r: (0, ex, b, 0))],
        out_specs=pl.BlockSpec(_row_tiled_shape(n, d), lambda b, ex, idx_r, gate_r: (b, 0)),
    )
    return pl.pallas_call(
        functools.partial(_combine_kernel, cap=cap, chunks=chunks),
        grid_spec=grid_spec,
        out_shape=jax.ShapeDtypeStruct(_row_tiled_shape(batch * n, d), F32),
        compiler_params=_params(("arbitrary", "arbitrary"), 2 * n * d * 4 + 2 * cap * d * 4),
        name="moe_combine",
    )(idx.reshape(-1), gate.reshape(-1), y3)


def _resid_kernel(x_ref, m_ref, gate_ref, gf_ref, o_ref, *, final):
    x = x_ref[...] + gate_ref[...] * _load_row_tiled(m_ref, x_ref.shape[1] // LANES)
    o_ref[...] = _rms(x, gf_ref[...]) if final else x


def _residual(x2, moe3, mod, mod_row, g_final, final, tm):
    rows, d = x2.shape
    row = lambda i: (i, 0)
    return pl.pallas_call(
        functools.partial(_resid_kernel, final=final),
        grid=(rows // tm,),
        in_specs=[pl.BlockSpec((tm, d), row), pl.BlockSpec(_row_tiled_shape(tm, d), row),
                  pl.BlockSpec((None, None, 1, d), lambda i: (mod_row(i), 5, 0, 0)),
                  pl.BlockSpec((1, d), lambda i: (0, 0))],
        out_specs=pl.BlockSpec((tm, d), row),
        out_shape=jax.ShapeDtypeStruct((rows, d), F32),
        compiler_params=_params(("parallel",), 6 * tm * d * 4),
        name="moe_residual_final" if final else "moe_residual",
    )(x2, moe3, mod, g_final.reshape(1, d))


def _moe(h3, logits_t, batch, layer, w_gate, w_up, w_down):
    d = w_gate.shape[2]
    n = logits_t.shape[1] // batch
    cap = max(1, CAPACITY_FACTOR * n // N_EXPERTS)
    idx, gate = _route(logits_t, batch, cap)
    xe = _gather(idx, h3, d, batch, cap)
    y3 = _expert_ffn(xe, layer, w_gate, w_up, w_down, FFN_ROW_TILE)
    return _combine(idx, gate, y3, d, batch, n, cap)


def _rope_tables(n):
    t = np.arange(n)
    n_freq = HEAD_DIM // 4
    inv = ROPE_THETA ** (-np.arange(n_freq, dtype=np.float64) / n_freq)
    ang_r = (t // GRID_W)[:, None] * inv[None, :]
    ang_c = (t % GRID_W)[:, None] * inv[None, :]
    ang = np.concatenate([ang_r, ang_r, ang_c, ang_c], axis=1)
    sign = np.concatenate([-np.ones(n_freq), np.ones(n_freq)] * 2)[None, :]
    reps = LANES // HEAD_DIM
    cos = np.tile(np.cos(ang), (1, reps))
    sin = np.tile(np.sin(ang) * sign, (1, reps))
    return jnp.asarray(cos, F32), jnp.asarray(sin, F32)


def kernel(x, c, ctx, c_ctx, w_ada, b_ada, g_mix, g_ffn, w_in, g_q, g_k, rel_bias, g_out_a, g_out_b, g_out_c, w_out, w_router, w_gate, w_up, w_down, g_final):
    batch, seq, d = x.shape
    ctx_len = ctx.shape[1]
    depth = w_ada.shape[0]
    b_w = g_out_b.shape[1]
    group_dim = b_w // B_GROUPS
    ctx_row = batch
    assert batch < MOD_ROWS and seq % ROW_TILE == 0

    cvec = jnp.zeros((MOD_ROWS, d), F32).at[:batch].set(c).at[ctx_row].set(c_ctx)
    mod_all = _ada(cvec, w_ada, b_ada).reshape(depth, MOD_ROWS, 6, 1, d)

    rope_tabs = _rope_tables(seq)
    blk = np.arange(LANES) // HEAD_DIM
    ones_hd = jnp.asarray(blk[:, None] == blk[None, :], dtype=BF16)
    cs = _channel_dft(b_w, group_dim)
    dft_lat = _dft_tables(seq, group_dim)
    dft_ctx = _dft_tables(ctx_len, group_dim)

    lat_tile = ROW_TILE
    ctx_tile = min(ROW_TILE, ctx_len)
    lat_row = lambda i: i // (seq // lat_tile)
    ctx_mod_row = lambda i: ctx_row
    group_a = A_Q_HEADS // A_KV_HEADS

    xl = x.reshape(batch * seq, d)
    xc = ctx.reshape(batch * ctx_len, d)
    for l in range(depth):
        last = l == depth - 1
        mod = mod_all[l]
        w_in_bf = w_in[l].astype(BF16)
        w_out_bf = w_out[l].astype(BF16)
        w_router_t = _router_split(w_router[l])
        gq_t = jnp.tile(g_q[l], LANES // HEAD_DIM).reshape(1, LANES)
        gk_t = jnp.tile(g_k[l], LANES // HEAD_DIM).reshape(1, LANES)

        qa, ka, va_t, yc, ys, qn, kn, vn = _inproj(xl, mod, lat_row, g_mix[l], w_in_bf, gq_t, gk_t, ones_hd, cs,
                                                   rope_tabs, seq, lat_tile)
        qa_c, ka_c, va_c_t, yc_c, ys_c, qn_c, kn_c, vn_c, vn_c_t = _inproj(
            xc, mod, ctx_mod_row, g_mix[l], w_in_bf, gq_t, gk_t, ones_hd, cs, None, ctx_len, ctx_tile)
        o_a_t = _attention(qa, ka, va_t, ka_c, va_c_t, batch, A_KV_HEADS, group_a, ATTN_Q_TILE)
        o_b = _fourier(dft_lat[0], dft_lat[1], yc, ys, batch, FOURIER_ROW_TILE)
        o_c = _natten(qn, kn, vn, kn_c, vn_c, rel_bias[l], batch)
        x_mid, h2, logits_t = _merge(o_a_t, o_b, o_c, False, xl, mod, lat_row, g_out_a[l], g_out_b[l],
                                     g_out_c[l], w_out_bf, g_ffn[l], w_router_t, lat_tile)
        moe = _moe(h2, logits_t, batch, l, w_gate, w_up, w_down)
        xl = _residual(x_mid, moe, mod, lat_row, g_final, last, lat_tile)
        if not last:
            o_a_c_t = _attention(qa_c, ka_c, va_c_t, None, None, batch, A_KV_HEADS, group_a, ctx_len)
            o_b_c = _fourier(dft_ctx[0], dft_ctx[1], yc_c, ys_c, batch, ctx_len)
            o_c_c_t = _attention(qn_c, kn_c, vn_c_t, None, None, batch, C_HEADS, 1, ctx_len)
            xc_mid, h2_c, logits_c = _merge(o_a_c_t, o_b_c, o_c_c_t, True, xc, mod, ctx_mod_row, g_out_a[l],
                                            g_out_b[l], g_out_c[l], w_out_bf, g_ffn[l], w_router_t, ctx_tile)
            moe_c = _moe(h2_c, logits_c, batch, l, w_gate, w_up, w_down)
            xc = _residual(xc_mid, moe_c, mod, ctx_mod_row, g_final, False, ctx_tile)
    return xl.reshape(batch, seq, d)
```

```python
import functools
import math

import numpy as np
import jax
import jax.numpy as jnp
from jax import lax
from jax.experimental import pallas as pl
from jax.experimental.pallas import tpu as pltpu

F32 = jnp.float32
BF16 = jnp.bfloat16
HIGHEST = lax.Precision.HIGHEST

GRID_W = 64
HEAD_DIM = 64
A_Q_HEADS = 8
A_KV_HEADS = 2
B_GROUPS = 4
C_HEADS = 4
NA_WIN_R = 8
NA_WIN_C = 16
ROPE_THETA = 10000.0
N_EXPERTS = 16
CAPACITY_FACTOR = 2
EPS = 1e-6
MOD_ROWS = 16

LANES = 128
SUBLANES = 8
VMEM_BYTES_V7X = 64 * 1024 * 1024

ROW_TILE = 512
ATTN_Q_TILE = 256
ATTN_KEY_CHUNK = 512
ATTN_SCORE_SLOTS = 2
FOURIER_ROW_TILE = 512
NA_Q_ROWS = 4
FFN_ROW_TILE = 512
FFN_F_CHUNK = 512
NEG_BIG = -1e30


def _vmem_limit(nbytes):
    return int(min(max(2 * nbytes, 32 * 1024 * 1024), VMEM_BYTES_V7X - 8 * 1024 * 1024))


def _params(sem, nbytes):
    return pltpu.CompilerParams(dimension_semantics=sem, vmem_limit_bytes=_vmem_limit(nbytes))


def _rms(x, g):
    return x * lax.rsqrt(jnp.mean(x * x, axis=-1, keepdims=True) + EPS) * g


def _row_tiled_shape(rows, d):
    assert d % LANES == 0
    return (rows * (d // LANES), LANES)


def _store_row_tiled(ref, value):
    rows, d = value.shape
    chunks = d // LANES
    for c in range(chunks):
        ref[pl.ds(c, rows, stride=chunks), :] = value[:, c * LANES:(c + 1) * LANES]


def _load_row_tiled(ref, chunks):
    rows = ref.shape[0] // chunks
    return jnp.concatenate([ref[pl.ds(c, rows, stride=chunks), :] for c in range(chunks)], axis=1)


def _token_tile(i, chunks):
    return pl.ds(pl.multiple_of(i * chunks, chunks), chunks)


def _ada_kernel(c_ref, w_ref, b_ref, o_ref):
    c = c_ref[...]
    sc = c / (1.0 + jnp.exp(-c))
    o_ref[...] = jnp.dot(sc, w_ref[...], precision=HIGHEST, preferred_element_type=F32) + b_ref[...]


def _ada(cvec, w_ada, b_ada):
    depth, d, n6 = w_ada.shape
    tn = 1024
    return pl.pallas_call(
        _ada_kernel,
        grid=(depth, n6 // tn),
        in_specs=[
            pl.BlockSpec((MOD_ROWS, d), lambda l, j: (0, 0)),
            pl.BlockSpec((None, d, tn), lambda l, j: (l, 0, j)),
            pl.BlockSpec((None, 1, tn), lambda l, j: (l, 0, j)),
        ],
        out_specs=pl.BlockSpec((None, MOD_ROWS, tn), lambda l, j: (l, 0, j)),
        out_shape=jax.ShapeDtypeStruct((depth, MOD_ROWS, n6), F32),
        compiler_params=_params(("arbitrary", "arbitrary"), 2 * d * tn * 4),
        name="ada_mod",
    )(cvec, w_ada, b_ada.reshape(depth, 1, n6))


def _head_norm(t, ones, g):
    ssq = jnp.dot((t * t).astype(BF16), ones, preferred_element_type=F32)
    return t * lax.rsqrt(ssq * (1.0 / HEAD_DIM) + EPS) * g


def _rope(t, cos, sin):
    lane = lax.broadcasted_iota(jnp.int32, t.shape, 1)
    first = (lane & 31) < 16
    partner = jnp.where(first, pltpu.roll(t, LANES - 16, 1), pltpu.roll(t, 16, 1))
    return t * cos + partner * sin


def _inproj_kernel(*refs, rope, widths):
    a_w, kv_w, b_w, c_w = widths
    if rope:
        (x_ref, g_ref, shift_ref, scale_ref, w_ref, gq_ref, gk_ref, ones_ref, cs_ref, cos_ref, sin_ref,
         qa_ref, ka_ref, va_ref, yc_ref, ys_ref, qn_ref, kn_ref, vn_ref) = refs
        cos, sin = cos_ref[...], sin_ref[...]
    else:
        (x_ref, g_ref, shift_ref, scale_ref, w_ref, gq_ref, gk_ref, ones_ref, cs_ref,
         qa_ref, ka_ref, va_ref, yc_ref, ys_ref, qn_ref, kn_ref, vn_ref, vnt_ref) = refs
        cos = sin = None
    h = _rms(x_ref[...], g_ref[...]) * (1.0 + scale_ref[...]) + shift_ref[...]
    p = jnp.dot(h.astype(BF16), w_ref[...], preferred_element_type=F32)
    ones = ones_ref[...]
    q_scale = HEAD_DIM ** -0.5
    for j in range(a_w // LANES):
        t = _head_norm(p[:, j * LANES:(j + 1) * LANES], ones, gq_ref[...])
        if rope:
            t = _rope(t, cos, sin)
        qa_ref[:, j * LANES:(j + 1) * LANES] = (t * q_scale).astype(BF16)
    o = a_w
    t = _head_norm(p[:, o:o + kv_w], ones, gk_ref[...])
    if rope:
        t = _rope(t, cos, sin)
    ka_ref[...] = t.astype(BF16)
    o += kv_w
    va_ref[...] = jnp.transpose(p[:, o:o + kv_w]).astype(BF16)
    o += kv_w
    y = jnp.dot(p[:, o:o + b_w].astype(BF16), cs_ref[...], preferred_element_type=F32)
    yc_ref[...] = y[:, :b_w].astype(BF16)
    ys_ref[...] = y[:, b_w:].astype(BF16)
    o += b_w
    qn_ref[...] = (p[:, o:o + c_w] * q_scale).astype(BF16)
    o += c_w
    kn_ref[...] = p[:, o:o + c_w].astype(BF16)
    o += c_w
    vn_ref[...] = p[:, o:o + c_w].astype(BF16)
    if not rope:
        vnt_ref[...] = jnp.transpose(p[:, o:o + c_w]).astype(BF16)


def _inproj(x2, mod, mod_row, g_mix, w_in_bf, gq_t, gk_t, ones_hd, cs, rope_tabs, seq, tm):
    rows, d = x2.shape
    in_w = w_in_bf.shape[1]
    a_w = A_Q_HEADS * HEAD_DIM
    kv_w = A_KV_HEADS * HEAD_DIM
    b_w = cs.shape[0]
    c_w = C_HEADS * HEAD_DIM
    assert kv_w == LANES and in_w == a_w + 2 * kv_w + b_w + 3 * c_w
    rope = rope_tabs is not None
    steps_per_seq = seq // tm
    const = lambda i: (0, 0)
    row = lambda i: (i, 0)
    in_specs = [
        pl.BlockSpec((tm, d), row),
        pl.BlockSpec((1, d), const),
        pl.BlockSpec((None, None, 1, d), lambda i: (mod_row(i), 0, 0, 0)),
        pl.BlockSpec((None, None, 1, d), lambda i: (mod_row(i), 1, 0, 0)),
        pl.BlockSpec((d, in_w), const),
        pl.BlockSpec((1, LANES), const),
        pl.BlockSpec((1, LANES), const),
        pl.BlockSpec((LANES, LANES), const),
        pl.BlockSpec((b_w, 2 * b_w), const),
    ]
    args = [x2, g_mix.reshape(1, d), mod, mod, w_in_bf, gq_t, gk_t, ones_hd, cs]
    if rope:
        in_specs += [pl.BlockSpec((tm, LANES), lambda i: (i % steps_per_seq, 0))] * 2
        args += list(rope_tabs)
    out_w = [a_w, kv_w, kv_w, b_w, b_w, c_w, c_w, c_w]
    out_specs = [pl.BlockSpec((tm, w), row) for w in out_w]
    out_shape = [jax.ShapeDtypeStruct((rows, w), BF16) for w in out_w]
    out_specs[2] = pl.BlockSpec((kv_w, tm), lambda i: (0, i))
    out_shape[2] = jax.ShapeDtypeStruct((kv_w, rows), BF16)
    if not rope:
        out_specs.append(pl.BlockSpec((c_w, tm), lambda i: (0, i)))
        out_shape.append(jax.ShapeDtypeStruct((c_w, rows), BF16))
    return pl.pallas_call(
        functools.partial(_inproj_kernel, rope=rope, widths=(a_w, kv_w, b_w, c_w)),
        grid=(rows // tm,),
        in_specs=in_specs,
        out_specs=out_specs,
        out_shape=out_shape,
        compiler_params=_params(("parallel",), 2 * d * in_w * 2 + 2 * tm * d * 4 + 3 * tm * in_w * 4),
        name="inproj_rope" if rope else "inproj",
    )(*args)


def _place_head(q_ref, h, kv, kw):
    assert 2 * HEAD_DIM == LANES
    src_blk, src_off = divmod(h * HEAD_DIM, LANES)
    dst_blk, dst_off = divmod(kv * HEAD_DIM, LANES)
    blk = q_ref[:, src_blk * LANES:(src_blk + 1) * LANES].astype(F32)
    if src_off != dst_off:
        blk = pltpu.roll(blk, HEAD_DIM, 1)
    lane = lax.broadcasted_iota(jnp.int32, blk.shape, 1)
    keep = (lane >= dst_off) & (lane < dst_off + HEAD_DIM)
    blk = jnp.transpose(jnp.where(keep, blk, 0.0)).astype(BF16)
    parts = [blk if j == dst_blk else jnp.zeros_like(blk) for j in range(kw // LANES)]
    return parts[0] if len(parts) == 1 else jnp.concatenate(parts, axis=0)


def _attn_kernel(*refs, n_kv, group, two):
    if two:
        q_ref, k1_ref, v1_ref, k2_ref, v2_ref, o_ref, s_scr = refs
        segments = ((k1_ref, v1_ref), (k2_ref, v2_ref))
    else:
        q_ref, k1_ref, v1_ref, o_ref, s_scr = refs
        segments = ((k1_ref, v1_ref),)
    kw = k1_ref.shape[1]
    tq = q_ref.shape[0]
    dn = (((1,), (1,)), ((), ()))
    sum_rows = 2 * SUBLANES
    n_heads = n_kv * group
    chunks, base = [], 0
    for seg, (k_ref, _) in enumerate(segments):
        n_seg = k_ref.shape[0]
        for c0 in range(0, n_seg, ATTN_KEY_CHUNK):
            chunks.append((seg, c0, min(ATTN_KEY_CHUNK, n_seg - c0), base + c0))
        base += n_seg

    def scores(h):
        slot = h % s_scr.shape[0]
        qp = _place_head(q_ref, h, h // group, kw)
        m = None
        for seg, c0, size, row in chunks:
            s = jnp.dot(segments[seg][0][c0:c0 + size, :], qp, preferred_element_type=F32)
            s_scr[slot, row:row + size, :] = s
            mc = jnp.max(s, axis=0, keepdims=True)
            m = mc if m is None else jnp.maximum(m, mc)
        return m

    def finish(h, m):
        slot, kv = h % s_scr.shape[0], h // group
        vs = slice(kv * HEAD_DIM, (kv + 1) * HEAD_DIM)
        acc = jnp.zeros((HEAD_DIM + sum_rows, tq), F32)
        for seg, c0, size, row in chunks:
            va = jnp.concatenate([segments[seg][1][vs, c0:c0 + size], jnp.ones((sum_rows, size), BF16)], axis=0)
            p = jnp.exp((s_scr[slot, row:row + size, :] - m).astype(BF16))
            acc = acc + jnp.dot(va, p, preferred_element_type=F32)
        o = acc[:HEAD_DIM] * (1.0 / acc[HEAD_DIM:HEAD_DIM + 1])
        o_ref[h * HEAD_DIM:(h + 1) * HEAD_DIM, :] = o.astype(BF16)

    n_slots = s_scr.shape[0]
    maxes = [scores(h) for h in range(min(n_slots - 1, n_heads))]
    for h in range(n_heads):
        if h + n_slots - 1 < n_heads:
            maxes.append(scores(h + n_slots - 1))
        finish(h, maxes[h])


def _attention(q, k1, v1t, k2, v2t, batch, n_kv, group, tq):
    rows, qw = q.shape
    n_q = rows // batch
    n1 = k1.shape[0] // batch
    kw = k1.shape[1]
    two = k2 is not None
    steps = n_q // tq
    in_specs = [
        pl.BlockSpec((tq, qw), lambda b, i: (b * steps + i, 0)),
        pl.BlockSpec((n1, kw), lambda b, i: (b, 0)),
        pl.BlockSpec((kw, n1), lambda b, i: (0, b)),
    ]
    args = [q, k1, v1t]
    n_keys = n1
    if two:
        n2 = k2.shape[0] // batch
        in_specs += [pl.BlockSpec((n2, kw), lambda b, i: (b, 0)), pl.BlockSpec((kw, n2), lambda b, i: (0, b))]
        args += [k2, v2t]
        n_keys += n2
    return pl.pallas_call(
        functools.partial(_attn_kernel, n_kv=n_kv, group=group, two=two),
        grid=(batch, steps),
        in_specs=in_specs,
        out_specs=pl.BlockSpec((qw, tq), lambda b, i: (0, b * steps + i)),
        out_shape=jax.ShapeDtypeStruct((qw, rows), BF16),
        scratch_shapes=[pltpu.VMEM((ATTN_SCORE_SLOTS, n_keys, tq), F32)],
        compiler_params=_params(("parallel", "arbitrary"), 4 * n_keys * kw * 2 + 4 * tq * n_keys * 4),
        name="attn_two_seg" if two else "attn_one_seg",
    )(*args)


def _fourier_kernel(c_ref, s_ref, cn_ref, sn_ref, yc_ref, ys_ref, rev_ref, o_ref):
    yc, ys = yc_ref[...], ys_ref[...]
    a = jnp.dot(c_ref[...], yc, preferred_element_type=F32)
    b = jnp.dot(s_ref[...], ys, preferred_element_type=F32)
    o_ref[0] = (a + b).astype(BF16)
    nxt = (jnp.dot(cn_ref[...], yc, preferred_element_type=F32)
           - jnp.dot(sn_ref[...], ys, preferred_element_type=F32))[0:1]
    mirrored = jnp.dot(rev_ref[...], (a - b).astype(BF16), preferred_element_type=F32)
    row = lax.broadcasted_iota(jnp.int32, mirrored.shape, 0)
    o_ref[1] = jnp.where(row == 0, nxt, mirrored).astype(BF16)


def _fourier(c_tab, s_tab, yc, ys, batch, tm):
    n = c_tab.shape[0]
    w = yc.shape[1]
    nblk = n // 2 // tm
    rev = np.zeros((tm, tm), np.float32)
    rev[np.arange(1, tm), tm - np.arange(1, tm)] = 1.0
    few = 2 * SUBLANES
    next_rows = tm // few
    return pl.pallas_call(
        _fourier_kernel,
        grid=(nblk, batch),
        in_specs=[
            pl.BlockSpec((tm, n), lambda i, b: (i, 0)),
            pl.BlockSpec((tm, n), lambda i, b: (i, 0)),
            pl.BlockSpec((few, n), lambda i, b: ((i + 1) * next_rows, 0)),
            pl.BlockSpec((few, n), lambda i, b: ((i + 1) * next_rows, 0)),
            pl.BlockSpec((n, w), lambda i, b: (b, 0)),
            pl.BlockSpec((n, w), lambda i, b: (b, 0)),
            pl.BlockSpec((tm, tm), lambda i, b: (0, 0)),
        ],
        out_specs=pl.BlockSpec((None, 2, None, tm, w), lambda i, b: (b, 0, i, 0, 0)),
        out_shape=jax.ShapeDtypeStruct((batch, 2, nblk, tm, w), BF16),
        compiler_params=_params(("arbitrary", "arbitrary"), 4 * tm * n * 2 + 4 * n * w * 2),
        name="fourier",
    )(c_tab, s_tab, c_tab, s_tab, yc, ys, jnp.asarray(rev, F32).astype(BF16))


def _dft_tables(n, group_dim):
    scale = 1.0 / math.sqrt(n * group_dim)
    r = int(round(math.sqrt(n)))
    if r * r != n:
        k = jnp.arange(n, dtype=jnp.int32)
        ang = ((k[:, None] * k[None, :]) % n).astype(F32) * (2.0 * math.pi / n)
        return (jnp.cos(ang) * scale).astype(BF16), (-jnp.sin(ang) * scale).astype(BF16)
    k = jnp.arange(n, dtype=jnp.int32)[:, None]
    t = jnp.arange(r, dtype=jnp.int32)[None, :]
    ang_a = ((k * t) % r).astype(F32) * (2.0 * math.pi / r)
    ang_b = ((k * t) % n).astype(F32) * (2.0 * math.pi / n)
    ca, sa = jnp.cos(ang_a)[:, :, None], jnp.sin(ang_a)[:, :, None]
    cb, sb = jnp.cos(ang_b)[:, None, :], jnp.sin(ang_b)[:, None, :]
    c = (ca * cb - sa * sb) * scale
    s = (sa * cb + ca * sb) * (-scale)
    return c.reshape(n, n).astype(BF16), s.reshape(n, n).astype(BF16)


def _channel_dft(width, group_dim):
    j = np.arange(group_dim)
    ang = 2.0 * np.pi * ((j[:, None] * j[None, :]) % group_dim) / group_dim
    eye = np.eye(width // group_dim)
    cs = np.concatenate([np.kron(eye, np.cos(ang)), np.kron(eye, np.sin(ang))], axis=1)
    return jnp.asarray(cs, dtype=F32).astype(BF16)


def _na_kernel(ws_ref, var_ref, q_ref, k_ref, v_ref, kc_ref, vc_ref, bias_ref, o_ref, *, band):
    i = pl.program_id(1)
    start = pl.multiple_of(ws_ref[i] * GRID_W, GRID_W)
    var = var_ref[i]
    kw = k_ref[pl.ds(start, band), :]
    vw = v_ref[pl.ds(start, band), :]
    kc, vc = kc_ref[...], vc_ref[...]
    q = q_ref[...]
    dn = (((1,), (1,)), ((), ()))
    head_of_lane = lax.broadcasted_iota(jnp.int32, q.shape, 1) // HEAD_DIM
    out = jnp.zeros(q.shape, F32)
    for h in range(C_HEADS):
        mine = head_of_lane == h
        qh = jnp.where(mine, q, jnp.zeros_like(q))
        s1 = lax.dot_general(qh, kw, dn, preferred_element_type=F32) + bias_ref[var, h].astype(F32)
        s2 = lax.dot_general(qh, kc, dn, preferred_element_type=F32)
        m = jnp.maximum(jnp.max(s1, axis=-1, keepdims=True), jnp.max(s2, axis=-1, keepdims=True))
        p1 = jnp.exp((s1 - m).astype(BF16))
        p2 = jnp.exp((s2 - m).astype(BF16))
        l = jnp.sum(p1.astype(F32), axis=-1, keepdims=True) + jnp.sum(p2.astype(F32), axis=-1, keepdims=True)
        o = jnp.dot(p1, vw, preferred_element_type=F32) + jnp.dot(p2, vc, preferred_element_type=F32)
        out = jnp.where(mine, o * (1.0 / l), out)
    o_ref[...] = out.astype(BF16)


def _na_plan(seq):
    rows = seq // GRID_W
    wr = min(NA_WIN_R, rows)
    wc = min(NA_WIN_C, GRID_W)
    qr = min(NA_Q_ROWS, rows)
    band_rows = min(rows, 2 * ((qr + wr) // 2))
    nblk = rows // qr
    n_row_off, n_col_off = 2 * NA_WIN_R - 1, 2 * NA_WIN_C - 1
    c = np.arange(GRID_W)
    cs = np.clip(c - wc // 2, 0, GRID_W - wc)
    col_ok = (c[None, :] >= cs[:, None]) & (c[None, :] < cs[:, None] + wc)
    col_idx = np.where(col_ok, c[None, :] - c[:, None] + (NA_WIN_C - 1), n_col_off).astype(np.int32)
    ws_list, var_list, variants = [], [], []
    for blk in range(nblk):
        r = blk * qr + np.arange(qr)
        rs = np.clip(r - wr // 2, 0, rows - wr)
        ws = int(np.clip(rs[0], 0, rows - band_rows))
        kr = ws + np.arange(band_rows)
        row_ok = (kr[None, :] >= rs[:, None]) & (kr[None, :] < rs[:, None] + wr)
        assert row_ok.sum(axis=1).min() == wr
        idx = np.where(row_ok, kr[None, :] - r[:, None] + (NA_WIN_R - 1), n_row_off).astype(np.int32)
        for v, known in enumerate(variants):
            if np.array_equal(known, idx):
                break
        else:
            v = len(variants)
            variants.append(idx)
        ws_list.append(ws)
        var_list.append(v)
    return (np.asarray(ws_list, np.int32), np.asarray(var_list, np.int32),
            np.stack(variants), col_idx, qr, band_rows)


def _natten(qn, kn, vn, kc, vc, rel_bias_l, batch):
    rows, w = qn.shape
    seq = rows // batch
    ctx_len = kc.shape[0] // batch
    ws, var, row_idx, col_idx, qr, band_rows = _na_plan(seq)
    nblk = ws.shape[0]
    tq, band = qr * GRID_W, band_rows * GRID_W
    nvar = row_idx.shape[0]
    padded = jnp.pad(rel_bias_l, ((0, 0), (0, 1), (0, 1)), constant_values=NEG_BIG)
    tiles = padded[:, :, col_idx].astype(BF16)
    bias = jnp.take(tiles, jnp.asarray(row_idx.reshape(-1)), axis=1)
    bias = bias.reshape(C_HEADS, nvar, qr, band_rows, GRID_W, GRID_W)
    bias = jnp.transpose(bias, (1, 0, 2, 4, 3, 5)).reshape(nvar, C_HEADS, tq, band)
    grid_spec = pltpu.PrefetchScalarGridSpec(
        num_scalar_prefetch=2,
        grid=(batch, nblk),
        in_specs=[
            pl.BlockSpec((tq, w), lambda b, i, ws_r, var_r: (b * nblk + i, 0)),
            pl.BlockSpec((seq, w), lambda b, i, ws_r, var_r: (b, 0)),
            pl.BlockSpec((seq, w), lambda b, i, ws_r, var_r: (b, 0)),
            pl.BlockSpec((ctx_len, w), lambda b, i, ws_r, var_r: (b, 0)),
            pl.BlockSpec((ctx_len, w), lambda b, i, ws_r, var_r: (b, 0)),
            pl.BlockSpec((nvar, C_HEADS, tq, band), lambda b, i, ws_r, var_r: (0, 0, 0, 0)),
        ],
        out_specs=pl.BlockSpec((tq, w), lambda b, i, ws_r, var_r: (b * nblk + i, 0)),
    )
    nbytes = 2 * nvar * C_HEADS * tq * band * 2 + 4 * seq * w * 2 + 6 * tq * (band + ctx_len) * 4
    return pl.pallas_call(
        functools.partial(_na_kernel, band=band),
        grid_spec=grid_spec,
        out_shape=jax.ShapeDtypeStruct((rows, w), BF16),
        compiler_params=_params(("parallel", "arbitrary"), nbytes),
        name="natten",
    )(jnp.asarray(ws), jnp.asarray(var), qn, kn, vn, kc, vc, bias)


def _merge_kernel(oa_ref, ob_ref, oc_ref, x_ref, gate_ref, shift_ref, scale_ref, ga_ref, gb_ref, gc_ref,
                  w_ref, gf_ref, wa_ref, wb_ref, xo_ref, h_ref, lg_ref, *, c_feature_major):
    def normed(o_ref, g_ref):
        return _rms(o_ref[...].astype(F32), g_ref[...]).astype(BF16)

    def normed_t(o_ref, g_ref):
        o = o_ref[...].astype(F32)
        return (o * lax.rsqrt(jnp.mean(o * o, axis=0, keepdims=True) + EPS) * g_ref[...]).astype(BF16)

    dn_t = (((0,), (0,)), ((), ()))
    a_w, b_w = oa_ref.shape[0], ob_ref.shape[1]
    y = lax.dot_general(normed_t(oa_ref, ga_ref), w_ref[0:a_w, :], dn_t, preferred_element_type=F32)
    y = y + jnp.dot(normed(ob_ref, gb_ref), w_ref[a_w:a_w + b_w, :], preferred_element_type=F32)
    if c_feature_major:
        y = y + lax.dot_general(normed_t(oc_ref, gc_ref), w_ref[a_w + b_w:, :], dn_t,
                                preferred_element_type=F32)
    else:
        y = y + jnp.dot(normed(oc_ref, gc_ref), w_ref[a_w + b_w:, :], preferred_element_type=F32)
    x = x_ref[...] + gate_ref[...] * y
    xo_ref[...] = x
    h = _rms(x, gf_ref[...]) * (1.0 + scale_ref[...]) + shift_ref[...]
    _store_row_tiled(h_ref, h)
    h_hi = h.astype(BF16)
    h_lo = (h - h_hi.astype(F32)).astype(BF16)
    t = jnp.dot(h_hi, wa_ref[...], preferred_element_type=F32) + jnp.dot(h_lo, wb_ref[...], preferred_element_type=F32)
    t = jnp.transpose(t)
    lg_ref[...] = t[0:N_EXPERTS] + t[N_EXPERTS:2 * N_EXPERTS]


def _router_split(w_router_l):
    d, n_exp = w_router_l.shape
    w_hi = w_router_l.astype(BF16)
    w_lo = (w_router_l - w_hi.astype(F32)).astype(BF16)
    wa = jnp.concatenate([w_hi, w_lo, jnp.zeros((d, LANES - 2 * n_exp), BF16)], axis=1)
    wb = jnp.concatenate([w_hi, jnp.zeros((d, LANES - n_exp), BF16)], axis=1)
    return wa, wb


def _merge(oa_t, ob, oc, c_feature_major, x2, mod, mod_row, g_a, g_b, g_c, w_out_bf, g_ffn, w_router_ab, tm):
    rows, d = x2.shape
    a_w, b_w = oa_t.shape[0], ob.shape[-1]
    nblk = ob.shape[2]
    assert ob.shape[3] == tm or nblk == 1

    def ob_index(i):
        tile = i % (2 * nblk)
        return (i // (2 * nblk), tile // nblk, jnp.where(tile < nblk, tile, 2 * nblk - 1 - tile), 0, 0)

    if nblk == 1 and ob.shape[3] != tm:
        ob = ob.reshape(-1, b_w)
        ob_spec = pl.BlockSpec((tm, b_w), lambda i: (i, 0))
    else:
        ob_spec = pl.BlockSpec((None, None, None, tm, b_w), ob_index)
    row = lambda i: (i, 0)
    col = lambda i: (0, i)
    const = lambda i: (0, 0)
    mod_spec = lambda j: pl.BlockSpec((None, None, 1, d), lambda i: (mod_row(i), j, 0, 0))
    if c_feature_major:
        c_w = oc.shape[0]
        oc_spec, gc_spec, g_c2 = pl.BlockSpec((c_w, tm), col), pl.BlockSpec((c_w, 1), const), g_c.reshape(-1, 1)
    else:
        c_w = oc.shape[1]
        oc_spec, gc_spec, g_c2 = pl.BlockSpec((tm, c_w), row), pl.BlockSpec((1, c_w), const), g_c.reshape(1, -1)
    return pl.pallas_call(
        functools.partial(_merge_kernel, c_feature_major=c_feature_major),
        grid=(rows // tm,),
        in_specs=[
            pl.BlockSpec((a_w, tm), col), ob_spec, oc_spec,
            pl.BlockSpec((tm, d), row),
            mod_spec(2), mod_spec(3), mod_spec(4),
            pl.BlockSpec((a_w, 1), const), pl.BlockSpec((1, b_w), const), gc_spec,
            pl.BlockSpec((a_w + b_w + c_w, d), const),
            pl.BlockSpec((1, d), const),
            pl.BlockSpec((d, LANES), const),
            pl.BlockSpec((d, LANES), const),
        ],
        out_specs=[pl.BlockSpec((tm, d), row),
                   pl.BlockSpec(_row_tiled_shape(tm, d), row),
                   pl.BlockSpec((N_EXPERTS, tm), lambda i: (0, i))],
        out_shape=[jax.ShapeDtypeStruct((rows, d), F32), jax.ShapeDtypeStruct(_row_tiled_shape(rows, d), F32),
                   jax.ShapeDtypeStruct((N_EXPERTS, rows), F32)],
        compiler_params=_params(("parallel",), 2 * d * d * 2 + 8 * tm * d * 4),
        name="merge_out",
    )(oa_t, ob, oc, x2, mod, mod, mod, g_a.reshape(-1, 1), g_b.reshape(1, -1), g_c2,
      w_out_bf, g_ffn.reshape(1, d), *w_router_ab)


def _route_kernel(lg_ref, tri_ref, idx_ref, gate_ref, *, n, cap):
    lg = lg_ref[...]
    e = jnp.exp(lg - jnp.max(lg, axis=0, keepdims=True))
    aff = e / jnp.sum(e, axis=0, keepdims=True)
    fcap = float(cap)

    def enough(v):
        return jnp.sum(jnp.where(aff >= v, 1.0, 0.0), axis=1, keepdims=True) >= fcap

    p = jnp.full((N_EXPERTS, 1), 2.0, F32)
    for j in (64, 32, 16, 8, 4, 2, 1):
        cand = p * (2.0 ** -j)
        p = jnp.where(enough(cand), p, cand)
    base = p * 0.5
    mant = jnp.zeros((N_EXPERTS, 1), F32)
    for j in range(22, -1, -1):
        cand = mant + float(2 ** j)
        mant = jnp.where(enough(base * (1.0 + cand * (2.0 ** -23))), cand, mant)
    thr = base * (1.0 + mant * (2.0 ** -23))

    tri = tri_ref[...]

    def excl_cumsum(mask):
        out, off = [], jnp.zeros((N_EXPERTS, 1), F32)
        for c in range(n // LANES):
            mc = mask[:, c * LANES:(c + 1) * LANES]
            inc = jnp.dot(mc.astype(BF16), tri, preferred_element_type=F32)
            out.append(inc - mc + off)
            off = off + inc[:, LANES - 1:LANES]
        return jnp.concatenate(out, axis=1), off

    gt = jnp.where(aff > thr, 1.0, 0.0)
    eq = jnp.where(aff == thr, 1.0, 0.0)
    need = fcap - jnp.sum(gt, axis=1, keepdims=True)
    eq_rank, _ = excl_cumsum(eq)
    sel = jnp.maximum(gt, jnp.where(eq_rank < need, eq, 0.0))
    pos, _ = excl_cumsum(sel)

    chosen = sel > 0.0
    tok = lax.broadcasted_iota(jnp.int32, (N_EXPERTS, n), 1)
    dist = jnp.where(chosen, tok - pos.astype(jnp.int32), 0)
    tok1 = jnp.where(chosen, tok + 1, 0)
    gate = jnp.where(chosen, aff, 0.0)
    for bit in range((n - 1).bit_length()):
        left = n - (1 << bit)
        dist_in, tok_in, gate_in = pltpu.roll(dist, left, 1), pltpu.roll(tok1, left, 1), pltpu.roll(gate, left, 1)
        arrives = ((dist_in >> bit) & 1) == 1
        stays = ((dist >> bit) & 1) == 0
        tok1 = jnp.where(arrives, tok_in, jnp.where(stays, tok1, 0))
        gate = jnp.where(arrives, gate_in, jnp.where(stays, gate, 0.0))
        dist = jnp.where(arrives, dist_in, jnp.where(stays, dist, 0))
    idx_ref[...] = tok1[:, :cap] - 1
    gate_ref[...] = gate[:, :cap]


def _route(logits_t, batch, cap):
    n = logits_t.shape[1] // batch
    assert n & (n - 1) == 0 and n % LANES == 0
    tri = jnp.asarray(np.triu(np.ones((LANES, LANES))), dtype=BF16)
    return pl.pallas_call(
        functools.partial(_route_kernel, n=n, cap=cap),
        grid=(batch,),
        in_specs=[pl.BlockSpec((N_EXPERTS, n), lambda b: (0, b)),
                  pl.BlockSpec((LANES, LANES), lambda b: (0, 0))],
        out_specs=[pl.BlockSpec((None, N_EXPERTS, cap), lambda b: (b, 0, 0)),
                   pl.BlockSpec((None, N_EXPERTS, cap), lambda b: (b, 0, 0))],
        out_shape=[jax.ShapeDtypeStruct((batch, N_EXPERTS, cap), jnp.int32),
                   jax.ShapeDtypeStruct((batch, N_EXPERTS, cap), F32)],
        compiler_params=_params(("parallel",), 64 * N_EXPERTS * n * 4),
        name="route_topk",
    )(logits_t, tri)


def _gather_kernel(idx_ref, h_ref, xe_ref, rows_scr, *, cap, chunks):
    b, ex = pl.program_id(0), pl.program_id(1)
    base = (b * N_EXPERTS + ex) * cap
    unroll = 2 * SUBLANES

    def body(i, carry):
        rows = [h_ref[_token_tile(idx_ref[base + i * unroll + u], chunks), :] for u in range(unroll)]
        for u in range(unroll):
            rows_scr[_token_tile(i * unroll + u, chunks), :] = rows[u]
        return carry

    lax.fori_loop(0, cap // unroll, body, 0)
    xe_ref[...] = _load_row_tiled(rows_scr, chunks).astype(BF16)


def _gather(idx, h3, d, batch, cap):
    chunks = d // LANES
    n = h3.shape[0] // chunks // batch
    grid_spec = pltpu.PrefetchScalarGridSpec(
        num_scalar_prefetch=1,
        grid=(batch, N_EXPERTS),
        in_specs=[pl.BlockSpec(_row_tiled_shape(n, d), lambda b, ex, idx_r: (b, 0))],
        out_specs=pl.BlockSpec((None, cap, d), lambda b, ex, idx_r: (ex, b, 0)),
        scratch_shapes=[pltpu.VMEM(_row_tiled_shape(cap, d), F32)],
    )
    return pl.pallas_call(
        functools.partial(_gather_kernel, cap=cap, chunks=chunks),
        grid_spec=grid_spec,
        out_shape=jax.ShapeDtypeStruct((N_EXPERTS, batch * cap, d), BF16),
        compiler_params=_params(("arbitrary", "arbitrary"), 2 * n * d * 4 + 3 * cap * d * 4),
        name="moe_gather",
    )(idx.reshape(-1), h3)


def _ffn_kernel(x_ref, wg_ref, wu_ref, wd_ref, y_ref, *, f_chunk):
    x = x_ref[...]
    y = None
    for c in range(wg_ref.shape[1] // f_chunk):
        cs = slice(c * f_chunk, (c + 1) * f_chunk)
        g = jnp.dot(x, wg_ref[:, cs], preferred_element_type=F32)
        u = jnp.dot(x, wu_ref[:, cs], preferred_element_type=F32)
        a = (g / (1.0 + jnp.exp(-g)) * u).astype(BF16)
        part = jnp.dot(a, wd_ref[cs, :], preferred_element_type=F32)
        y = part if y is None else y + part
    _store_row_tiled(y_ref, y)


def _expert_ffn(xe, layer, wg_bf, wu_bf, wd_bf, tm):
    n_exp, rows, d = xe.shape
    ff = wg_bf.shape[3]
    tm = min(tm, rows)
    return pl.pallas_call(
        functools.partial(_ffn_kernel, f_chunk=min(FFN_F_CHUNK, ff)),
        grid=(n_exp, rows // tm),
        in_specs=[
            pl.BlockSpec((None, tm, d), lambda ex, i: (ex, i, 0)),
            pl.BlockSpec((None, None, d, ff), lambda ex, i: (layer, ex, 0, 0)),
            pl.BlockSpec((None, None, d, ff), lambda ex, i: (layer, ex, 0, 0)),
            pl.BlockSpec((None, None, ff, d), lambda ex, i: (layer, ex, 0, 0)),
        ],
        out_specs=pl.BlockSpec((None,) + _row_tiled_shape(tm, d), lambda ex, i: (ex, i, 0)),
        out_shape=jax.ShapeDtypeStruct((n_exp,) + _row_tiled_shape(rows, d), F32),
        compiler_params=_params(("parallel", "arbitrary"), 2 * 3 * d * ff * 2 + 8 * tm * d * 4),
        name="expert_ffn",
    )(xe, wg_bf, wu_bf, wd_bf)


def _combine_kernel(idx_ref, gate_ref, y_ref, acc_ref, *, cap, chunks):
    b, ex = pl.program_id(0), pl.program_id(1)
    base = (b * N_EXPERTS + ex) * cap

    @pl.when(ex == 0)
    def _():
        acc_ref[...] = jnp.zeros_like(acc_ref)

    unroll = SUBLANES

    def body(i, carry):
        js = [i * unroll + u for u in range(unroll)]
        rs = [_token_tile(idx_ref[base + j], chunks) for j in js]
        new = [acc_ref[r, :] + y_ref[_token_tile(j, chunks), :] * gate_ref[base + j] for r, j in zip(rs, js)]
        for r, v in zip(rs, new):
            acc_ref[r, :] = v
        return carry

    lax.fori_loop(0, cap // unroll, body, 0)


def _combine(idx, gate, y3, d, batch, n, cap):
    chunks = d // LANES
    grid_spec = pltpu.PrefetchScalarGridSpec(
        num_scalar_prefetch=2,
        grid=(batch, N_EXPERTS),
        in_specs=[pl.BlockSpec((None,) + _row_tiled_shape(cap, d), lambda b, ex, idx_r, gate_r: (ex, b, 0))],
        out_specs=pl.BlockSpec(_row_tiled_shape(n, d), lambda b, ex, idx_r, gate_r: (b, 0)),
    )
    return pl.pallas_call(
        functools.partial(_combine_kernel, cap=cap, chunks=chunks),
        grid_spec=grid_spec,
        out_shape=jax.ShapeDtypeStruct(_row_tiled_shape(batch * n, d), F32),
        compiler_params=_params(("arbitrary", "arbitrary"), 2 * n * d * 4 + 2 * cap * d * 4),
        name="moe_combine",
    )(idx.reshape(-1), gate.reshape(-1), y3)


def _resid_kernel(x_ref, m_ref, gate_ref, gf_ref, o_ref, *, final):
    x = x_ref[...] + gate_ref[...] * _load_row_tiled(m_ref, x_ref.shape[1] // LANES)
    o_ref[...] = _rms(x, gf_ref[...]) if final else x


def _residual(x2, moe3, mod, mod_row, g_final, final, tm):
    rows, d = x2.shape
    row = lambda i: (i, 0)
    return pl.pallas_call(
        functools.partial(_resid_kernel, final=final),
        grid=(rows // tm,),
        in_specs=[pl.BlockSpec((tm, d), row), pl.BlockSpec(_row_tiled_shape(tm, d), row),
                  pl.BlockSpec((None, None, 1, d), lambda i: (mod_row(i), 5, 0, 0)),
                  pl.BlockSpec((1, d), lambda i: (0, 0))],
        out_specs=pl.BlockSpec((tm, d), row),
        out_shape=jax.ShapeDtypeStruct((rows, d), F32),
        compiler_params=_params(("parallel",), 6 * tm * d * 4),
        name="moe_residual_final" if final else "moe_residual",
    )(x2, moe3, mod, g_final.reshape(1, d))


def _moe(h3, logits_t, batch, layer, wg_bf, wu_bf, wd_bf):
    d = wg_bf.shape[2]
    n = logits_t.shape[1] // batch
    cap = max(1, CAPACITY_FACTOR * n // N_EXPERTS)
    idx, gate = _route(logits_t, batch, cap)
    xe = _gather(idx, h3, d, batch, cap)
    y3 = _expert_ffn(xe, layer, wg_bf, wu_bf, wd_bf, FFN_ROW_TILE)
    return _combine(idx, gate, y3, d, batch, n, cap)


def _rope_tables(n):
    t = np.arange(n)
    n_freq = HEAD_DIM // 4
    inv = ROPE_THETA ** (-np.arange(n_freq, dtype=np.float64) / n_freq)
    ang_r = (t // GRID_W)[:, None] * inv[None, :]
    ang_c = (t % GRID_W)[:, None] * inv[None, :]
    ang = np.concatenate([ang_r, ang_r, ang_c, ang_c], axis=1)
    sign = np.concatenate([-np.ones(n_freq), np.ones(n_freq)] * 2)[None, :]
    reps = LANES // HEAD_DIM
    cos = np.tile(np.cos(ang), (1, reps))
    sin = np.tile(np.sin(ang) * sign, (1, reps))
    return jnp.asarray(cos, F32), jnp.asarray(sin, F32)


def kernel(x, c, ctx, c_ctx, w_ada, b_ada, g_mix, g_ffn, w_in, g_q, g_k, rel_bias, g_out_a, g_out_b, g_out_c, w_out, w_router, w_gate, w_up, w_down, g_final):
    batch, seq, d = x.shape
    ctx_len = ctx.shape[1]
    depth = w_ada.shape[0]
    b_w = g_out_b.shape[1]
    group_dim = b_w // B_GROUPS
    ctx_row = batch
    assert batch < MOD_ROWS and seq % ROW_TILE == 0

    cvec = jnp.zeros((MOD_ROWS, d), F32).at[:batch].set(c).at[ctx_row].set(c_ctx)
    mod_all = _ada(cvec, w_ada, b_ada).reshape(depth, MOD_ROWS, 6, 1, d)

    rope_tabs = _rope_tables(seq)
    blk = np.arange(LANES) // HEAD_DIM
    ones_hd = jnp.asarray(blk[:, None] == blk[None, :], dtype=BF16)
    cs = _channel_dft(b_w, group_dim)
    dft_lat = _dft_tables(seq, group_dim)
    dft_ctx = _dft_tables(ctx_len, group_dim)

    lat_tile = ROW_TILE
    ctx_tile = min(ROW_TILE, ctx_len)
    lat_row = lambda i: i // (seq // lat_tile)
    ctx_mod_row = lambda i: ctx_row
    group_a = A_Q_HEADS // A_KV_HEADS

    wg_bf, wu_bf, wd_bf = w_gate.astype(BF16), w_up.astype(BF16), w_down.astype(BF16)
    xl = x.reshape(batch * seq, d)
    xc = ctx.reshape(batch * ctx_len, d)
    for l in range(depth):
        last = l == depth - 1
        mod = mod_all[l]
        w_in_bf = w_in[l].astype(BF16)
        w_out_bf = w_out[l].astype(BF16)
        w_router_t = _router_split(w_router[l])
        gq_t = jnp.tile(g_q[l], LANES // HEAD_DIM).reshape(1, LANES)
        gk_t = jnp.tile(g_k[l], LANES // HEAD_DIM).reshape(1, LANES)

        qa, ka, va_t, yc, ys, qn, kn, vn = _inproj(xl, mod, lat_row, g_mix[l], w_in_bf, gq_t, gk_t, ones_hd, cs,
                                                   rope_tabs, seq, lat_tile)
        qa_c, ka_c, va_c_t, yc_c, ys_c, qn_c, kn_c, vn_c, vn_c_t = _inproj(
            xc, mod, ctx_mod_row, g_mix[l], w_in_bf, gq_t, gk_t, ones_hd, cs, None, ctx_len, ctx_tile)
        o_a_t = _attention(qa, ka, va_t, ka_c, va_c_t, batch, A_KV_HEADS, group_a, ATTN_Q_TILE)
        o_b = _fourier(dft_lat[0], dft_lat[1], yc, ys, batch, min(FOURIER_ROW_TILE, seq // 2))
        o_c = _natten(qn, kn, vn, kn_c, vn_c, rel_bias[l], batch)
        x_mid, h2, logits_t = _merge(o_a_t, o_b, o_c, False, xl, mod, lat_row, g_out_a[l], g_out_b[l],
                                     g_out_c[l], w_out_bf, g_ffn[l], w_router_t, lat_tile)
        moe = _moe(h2, logits_t, batch, l, wg_bf, wu_bf, wd_bf)
        xl = _residual(x_mid, moe, mod, lat_row, g_final, last, lat_tile)
        if not last:
            o_a_c_t = _attention(qa_c, ka_c, va_c_t, None, None, batch, A_KV_HEADS, group_a, ctx_len)
            o_b_c = _fourier(dft_ctx[0], dft_ctx[1], yc_c, ys_c, batch, ctx_len // 2)
            o_c_c_t = _attention(qn_c, kn_c, vn_c_t, None, None, batch, C_HEADS, 1, ctx_len)
            xc_mid, h2_c, logits_c = _merge(o_a_c_t, o_b_c, o_c_c_t, True, xc, mod, ctx_mod_row, g_out_a[l],
                                            g_out_b[l], g_out_c[l], w_out_bf, g_ffn[l], w_router_t, ctx_tile)
            moe_c = _moe(h2_c, logits_c, batch, l, wg_bf, wu_bf, wd_bf)
            xc = _residual(xc_mid, moe_c, mod, ctx_mod_row, g_final, False, ctx_tile)
    return xl.reshape(batch, seq, d)
```

```python
import functools
import math

import numpy as np
import jax
import jax.numpy as jnp
from jax import lax
from jax.experimental import pallas as pl
from jax.experimental.pallas import tpu as pltpu

F32 = jnp.float32
BF16 = jnp.bfloat16
HIGHEST = lax.Precision.HIGHEST

GRID_W = 64
HEAD_DIM = 64
A_Q_HEADS = 8
A_KV_HEADS = 2
B_GROUPS = 4
C_HEADS = 4
NA_WIN_R = 8
NA_WIN_C = 16
ROPE_THETA = 10000.0
N_EXPERTS = 16
CAPACITY_FACTOR = 2
EPS = 1e-6
MOD_ROWS = 16

LANES = 128
SUBLANES = 8
VMEM_BYTES_V7X = 64 * 1024 * 1024

ROW_TILE = 512
ATTN_Q_TILE = 256
ATTN_KEY_CHUNK = 512
ATTN_SCORE_SLOTS = 2
FOURIER_ROW_TILE = 512
NA_Q_ROWS = 4
FFN_ROW_TILE = 512
FFN_F_CHUNK = 512
NEG_BIG = -1e30


def _vmem_limit(nbytes):
    return int(min(max(2 * nbytes, 32 * 1024 * 1024), VMEM_BYTES_V7X - 8 * 1024 * 1024))


def _params(sem, nbytes):
    return pltpu.CompilerParams(dimension_semantics=sem, vmem_limit_bytes=_vmem_limit(nbytes))


def _rms(x, g):
    return x * lax.rsqrt(jnp.mean(x * x, axis=-1, keepdims=True) + EPS) * g


def _row_tiled_shape(rows, d):
    assert d % LANES == 0
    return (rows * (d // LANES), LANES)


def _store_row_tiled(ref, value):
    rows, d = value.shape
    chunks = d // LANES
    for c in range(chunks):
        ref[pl.ds(c, rows, stride=chunks), :] = value[:, c * LANES:(c + 1) * LANES]


def _load_row_tiled(ref, chunks):
    rows = ref.shape[0] // chunks
    return jnp.concatenate([ref[pl.ds(c, rows, stride=chunks), :] for c in range(chunks)], axis=1)


def _token_tile(i, chunks):
    return pl.ds(pl.multiple_of(i * chunks, chunks), chunks)


def _ada_kernel(c_ref, w_ref, b_ref, o_ref):
    c = c_ref[...]
    sc = c / (1.0 + jnp.exp(-c))
    o_ref[...] = jnp.dot(sc, w_ref[...], precision=HIGHEST, preferred_element_type=F32) + b_ref[...]


def _ada(cvec, w_ada, b_ada):
    depth, d, n6 = w_ada.shape
    tn = 1024
    return pl.pallas_call(
        _ada_kernel,
        grid=(depth, n6 // tn),
        in_specs=[
            pl.BlockSpec((MOD_ROWS, d), lambda l, j: (0, 0)),
            pl.BlockSpec((None, d, tn), lambda l, j: (l, 0, j)),
            pl.BlockSpec((None, 1, tn), lambda l, j: (l, 0, j)),
        ],
        out_specs=pl.BlockSpec((None, MOD_ROWS, tn), lambda l, j: (l, 0, j)),
        out_shape=jax.ShapeDtypeStruct((depth, MOD_ROWS, n6), F32),
        compiler_params=_params(("arbitrary", "arbitrary"), 2 * d * tn * 4),
        name="ada_mod",
    )(cvec, w_ada, b_ada.reshape(depth, 1, n6))


def _head_norm(t, ones, g):
    ssq = jnp.dot((t * t).astype(BF16), ones, preferred_element_type=F32)
    return t * lax.rsqrt(ssq * (1.0 / HEAD_DIM) + EPS) * g


def _rope(t, cos, sin):
    lane = lax.broadcasted_iota(jnp.int32, t.shape, 1)
    first = (lane & 31) < 16
    partner = jnp.where(first, pltpu.roll(t, LANES - 16, 1), pltpu.roll(t, 16, 1))
    return t * cos + partner * sin


def _inproj_kernel(*refs, rope, widths):
    a_w, kv_w, b_w, c_w = widths
    if rope:
        (x_ref, g_ref, shift_ref, scale_ref, w_ref, gq_ref, gk_ref, ones_ref, cs_ref, cos_ref, sin_ref,
         qa_ref, ka_ref, va_ref, yc_ref, ys_ref, qn_ref, kn_ref, vn_ref) = refs
        cos, sin = cos_ref[...], sin_ref[...]
    else:
        (x_ref, g_ref, shift_ref, scale_ref, w_ref, gq_ref, gk_ref, ones_ref, cs_ref,
         qa_ref, ka_ref, va_ref, yc_ref, ys_ref, qn_ref, kn_ref, vn_ref, vnt_ref) = refs
        cos = sin = None
    h = _rms(x_ref[...], g_ref[...]) * (1.0 + scale_ref[...]) + shift_ref[...]
    p = jnp.dot(h.astype(BF16), w_ref[...], preferred_element_type=F32)
    ones = ones_ref[...]
    q_scale = HEAD_DIM ** -0.5
    for j in range(a_w // LANES):
        t = _head_norm(p[:, j * LANES:(j + 1) * LANES], ones, gq_ref[...])
        if rope:
            t = _rope(t, cos, sin)
        qa_ref[:, j * LANES:(j + 1) * LANES] = (t * q_scale).astype(BF16)
    o = a_w
    t = _head_norm(p[:, o:o + kv_w], ones, gk_ref[...])
    if rope:
        t = _rope(t, cos, sin)
    ka_ref[...] = t.astype(BF16)
    o += kv_w
    va_ref[...] = jnp.transpose(p[:, o:o + kv_w]).astype(BF16)
    o += kv_w
    y = jnp.dot(p[:, o:o + b_w].astype(BF16), cs_ref[...], preferred_element_type=F32)
    yc_ref[...] = y[:, :b_w].astype(BF16)
    ys_ref[...] = y[:, b_w:].astype(BF16)
    o += b_w
    qn_ref[...] = (p[:, o:o + c_w] * q_scale).astype(BF16)
    o += c_w
    kn_ref[...] = p[:, o:o + c_w].astype(BF16)
    o += c_w
    vn_ref[...] = p[:, o:o + c_w].astype(BF16)
    if not rope:
        vnt_ref[...] = jnp.transpose(p[:, o:o + c_w]).astype(BF16)


def _inproj(x2, mod, mod_row, g_mix, w_in_bf, gq_t, gk_t, ones_hd, cs, rope_tabs, seq, tm):
    rows, d = x2.shape
    in_w = w_in_bf.shape[1]
    a_w = A_Q_HEADS * HEAD_DIM
    kv_w = A_KV_HEADS * HEAD_DIM
    b_w = cs.shape[0]
    c_w = C_HEADS * HEAD_DIM
    assert kv_w == LANES and in_w == a_w + 2 * kv_w + b_w + 3 * c_w
    rope = rope_tabs is not None
    steps_per_seq = seq // tm
    const = lambda i: (0, 0)
    row = lambda i: (i, 0)
    in_specs = [
        pl.BlockSpec((tm, d), row),
        pl.BlockSpec((1, d), const),
        pl.BlockSpec((None, None, 1, d), lambda i: (mod_row(i), 0, 0, 0)),
        pl.BlockSpec((None, None, 1, d), lambda i: (mod_row(i), 1, 0, 0)),
        pl.BlockSpec((d, in_w), const),
        pl.BlockSpec((1, LANES), const),
        pl.BlockSpec((1, LANES), const),
        pl.BlockSpec((LANES, LANES), const),
        pl.BlockSpec((b_w, 2 * b_w), const),
    ]
    args = [x2, g_mix.reshape(1, d), mod, mod, w_in_bf, gq_t, gk_t, ones_hd, cs]
    if rope:
        in_specs += [pl.BlockSpec((tm, LANES), lambda i: (i % steps_per_seq, 0))] * 2
        args += list(rope_tabs)
    out_w = [a_w, kv_w, kv_w, b_w, b_w, c_w, c_w, c_w]
    out_specs = [pl.BlockSpec((tm, w), row) for w in out_w]
    out_shape = [jax.ShapeDtypeStruct((rows, w), BF16) for w in out_w]
    out_specs[2] = pl.BlockSpec((kv_w, tm), lambda i: (0, i))
    out_shape[2] = jax.ShapeDtypeStruct((kv_w, rows), BF16)
    if not rope:
        out_specs.append(pl.BlockSpec((c_w, tm), lambda i: (0, i)))
        out_shape.append(jax.ShapeDtypeStruct((c_w, rows), BF16))
    return pl.pallas_call(
        functools.partial(_inproj_kernel, rope=rope, widths=(a_w, kv_w, b_w, c_w)),
        grid=(rows // tm,),
        in_specs=in_specs,
        out_specs=out_specs,
        out_shape=out_shape,
        compiler_params=_params(("parallel",), 2 * d * in_w * 2 + 2 * tm * d * 4 + 3 * tm * in_w * 4),
        name="inproj_rope" if rope else "inproj",
    )(*args)


def _place_head(q_ref, h, kv, kw):
    assert 2 * HEAD_DIM == LANES
    src_blk, src_off = divmod(h * HEAD_DIM, LANES)
    dst_blk, dst_off = divmod(kv * HEAD_DIM, LANES)
    blk = q_ref[:, src_blk * LANES:(src_blk + 1) * LANES].astype(F32)
    if src_off != dst_off:
        blk = pltpu.roll(blk, HEAD_DIM, 1)
    lane = lax.broadcasted_iota(jnp.int32, blk.shape, 1)
    keep = (lane >= dst_off) & (lane < dst_off + HEAD_DIM)
    blk = jnp.transpose(jnp.where(keep, blk, 0.0)).astype(BF16)
    parts = [blk if j == dst_blk else jnp.zeros_like(blk) for j in range(kw // LANES)]
    return parts[0] if len(parts) == 1 else jnp.concatenate(parts, axis=0)


def _attn_kernel(*refs, n_kv, group, two, n_cast):
    n_in = 5 if two else 3
    cast_in, cast_out = refs[n_in:n_in + n_cast], refs[n_in + n_cast + 1:n_in + 2 * n_cast + 1]
    o_ref, s_scr = refs[n_in + n_cast], refs[-1]
    if two:
        q_ref, k1_ref, v1_ref, k2_ref, v2_ref = refs[:n_in]
        segments = ((k1_ref, v1_ref), (k2_ref, v2_ref))
    else:
        q_ref, k1_ref, v1_ref = refs[:n_in]
        segments = ((k1_ref, v1_ref),)
    for src, dst in zip(cast_in, cast_out):
        dst[...] = src[...].astype(BF16)
    kw = k1_ref.shape[1]
    tq = q_ref.shape[0]
    dn = (((1,), (1,)), ((), ()))
    sum_rows = 2 * SUBLANES
    n_heads = n_kv * group
    chunks, base = [], 0
    for seg, (k_ref, _) in enumerate(segments):
        n_seg = k_ref.shape[0]
        for c0 in range(0, n_seg, ATTN_KEY_CHUNK):
            chunks.append((seg, c0, min(ATTN_KEY_CHUNK, n_seg - c0), base + c0))
        base += n_seg

    def scores(h):
        slot = h % s_scr.shape[0]
        qp = _place_head(q_ref, h, h // group, kw)
        m = None
        for seg, c0, size, row in chunks:
            s = jnp.dot(segments[seg][0][c0:c0 + size, :], qp, preferred_element_type=F32)
            s_scr[slot, row:row + size, :] = s
            mc = jnp.max(s, axis=0, keepdims=True)
            m = mc if m is None else jnp.maximum(m, mc)
        return m

    def finish(h, m):
        slot, kv = h % s_scr.shape[0], h // group
        vs = slice(kv * HEAD_DIM, (kv + 1) * HEAD_DIM)
        acc = jnp.zeros((HEAD_DIM + sum_rows, tq), F32)
        for seg, c0, size, row in chunks:
            va = jnp.concatenate([segments[seg][1][vs, c0:c0 + size], jnp.ones((sum_rows, size), BF16)], axis=0)
            p = jnp.exp((s_scr[slot, row:row + size, :] - m).astype(BF16))
            acc = acc + jnp.dot(va, p, preferred_element_type=F32)
        o = acc[:HEAD_DIM] * (1.0 / acc[HEAD_DIM:HEAD_DIM + 1])
        o_ref[h * HEAD_DIM:(h + 1) * HEAD_DIM, :] = o.astype(BF16)

    n_slots = s_scr.shape[0]
    maxes = [scores(h) for h in range(min(n_slots - 1, n_heads))]
    for h in range(n_heads):
        if h + n_slots - 1 < n_heads:
            maxes.append(scores(h + n_slots - 1))
        finish(h, maxes[h])


def _attention(q, k1, v1t, k2, v2t, batch, n_kv, group, tq, cast=()):
    rows, qw = q.shape
    n_q = rows // batch
    n1 = k1.shape[0] // batch
    kw = k1.shape[1]
    two = k2 is not None
    steps = n_q // tq
    in_specs = [
        pl.BlockSpec((tq, qw), lambda b, i: (b * steps + i, 0)),
        pl.BlockSpec((n1, kw), lambda b, i: (b, 0)),
        pl.BlockSpec((kw, n1), lambda b, i: (0, b)),
    ]
    args = [q, k1, v1t]
    n_keys = n1
    if two:
        n2 = k2.shape[0] // batch
        in_specs += [pl.BlockSpec((n2, kw), lambda b, i: (b, 0)), pl.BlockSpec((kw, n2), lambda b, i: (0, b))]
        args += [k2, v2t]
        n_keys += n2
    out_specs = [pl.BlockSpec((qw, tq), lambda b, i: (0, b * steps + i))]
    out_shape = [jax.ShapeDtypeStruct((qw, rows), BF16)]
    n_steps = batch * steps
    for src, part, parts in cast:
        part_rows = src.shape[0] // parts
        slab = part_rows // n_steps
        assert slab * n_steps * parts == src.shape[0] and slab % (2 * SUBLANES) == 0
        in_specs.append(pl.BlockSpec((slab, src.shape[1]),
                                     lambda b, i, part=part: (part * n_steps + b * steps + i, 0)))
        out_specs.append(pl.BlockSpec((slab, src.shape[1]), lambda b, i: (b * steps + i, 0)))
        out_shape.append(jax.ShapeDtypeStruct((part_rows, src.shape[1]), BF16))
        args.append(src)
    outs = pl.pallas_call(
        functools.partial(_attn_kernel, n_kv=n_kv, group=group, two=two, n_cast=len(cast)),
        grid=(batch, steps),
        in_specs=in_specs,
        out_specs=out_specs,
        out_shape=out_shape,
        scratch_shapes=[pltpu.VMEM((ATTN_SCORE_SLOTS, n_keys, tq), F32)],
        compiler_params=_params(("parallel", "arbitrary"), 4 * n_keys * kw * 2 + 4 * tq * n_keys * 4),
        name="attn_two_seg" if two else "attn_one_seg",
    )(*args)
    return outs if cast else outs[0]


def _fourier_kernel(c_ref, s_ref, cn_ref, sn_ref, yc_ref, ys_ref, rev_ref, o_ref):
    yc, ys = yc_ref[...], ys_ref[...]
    a = jnp.dot(c_ref[...], yc, preferred_element_type=F32)
    b = jnp.dot(s_ref[...], ys, preferred_element_type=F32)
    o_ref[0] = (a + b).astype(BF16)
    nxt = (jnp.dot(cn_ref[...], yc, preferred_element_type=F32)
           - jnp.dot(sn_ref[...], ys, preferred_element_type=F32))[0:1]
    mirrored = jnp.dot(rev_ref[...], (a - b).astype(BF16), preferred_element_type=F32)
    row = lax.broadcasted_iota(jnp.int32, mirrored.shape, 0)
    o_ref[1] = jnp.where(row == 0, nxt, mirrored).astype(BF16)


def _fourier(c_tab, s_tab, yc, ys, batch, tm):
    n = c_tab.shape[1]
    w = yc.shape[1]
    nblk = n // 2 // tm
    rev = np.zeros((tm, tm), np.float32)
    rev[np.arange(1, tm), tm - np.arange(1, tm)] = 1.0
    few = 2 * SUBLANES
    next_rows = tm // few
    return pl.pallas_call(
        _fourier_kernel,
        grid=(nblk, batch),
        in_specs=[
            pl.BlockSpec((tm, n), lambda i, b: (i, 0)),
            pl.BlockSpec((tm, n), lambda i, b: (i, 0)),
            pl.BlockSpec((few, n), lambda i, b: ((i + 1) * next_rows, 0)),
            pl.BlockSpec((few, n), lambda i, b: ((i + 1) * next_rows, 0)),
            pl.BlockSpec((n, w), lambda i, b: (b, 0)),
            pl.BlockSpec((n, w), lambda i, b: (b, 0)),
            pl.BlockSpec((tm, tm), lambda i, b: (0, 0)),
        ],
        out_specs=pl.BlockSpec((None, 2, None, tm, w), lambda i, b: (b, 0, i, 0, 0)),
        out_shape=jax.ShapeDtypeStruct((batch, 2, nblk, tm, w), BF16),
        compiler_params=_params(("arbitrary", "arbitrary"), 4 * tm * n * 2 + 4 * n * w * 2),
        name="fourier",
    )(c_tab, s_tab, c_tab, s_tab, yc, ys, jnp.asarray(rev, F32).astype(BF16))


def _dft_tables(n, group_dim):
    scale = 1.0 / math.sqrt(n * group_dim)
    rows = n // 2 + 2 * SUBLANES
    r = int(round(math.sqrt(n)))
    if r * r != n:
        k = jnp.arange(rows, dtype=jnp.int32)
        t = jnp.arange(n, dtype=jnp.int32)
        ang = ((k[:, None] * t[None, :]) % n).astype(F32) * (2.0 * math.pi / n)
        return (jnp.cos(ang) * scale).astype(BF16), (-jnp.sin(ang) * scale).astype(BF16)
    k = jnp.arange(rows, dtype=jnp.int32)[:, None]
    t = jnp.arange(r, dtype=jnp.int32)[None, :]
    ang_a = ((k * t) % r).astype(F32) * (2.0 * math.pi / r)
    ang_b = ((k * t) % n).astype(F32) * (2.0 * math.pi / n)
    ca, sa = jnp.cos(ang_a)[:, :, None], jnp.sin(ang_a)[:, :, None]
    cb, sb = jnp.cos(ang_b)[:, None, :], jnp.sin(ang_b)[:, None, :]
    c = ((ca * cb - sa * sb) * scale).astype(BF16)
    s = ((sa * cb + ca * sb) * (-scale)).astype(BF16)
    return c.reshape(rows, n), s.reshape(rows, n)


def _channel_dft(width, group_dim):
    j = np.arange(group_dim)
    ang = 2.0 * np.pi * ((j[:, None] * j[None, :]) % group_dim) / group_dim
    eye = np.eye(width // group_dim)
    cs = np.concatenate([np.kron(eye, np.cos(ang)), np.kron(eye, np.sin(ang))], axis=1)
    return jnp.asarray(cs, dtype=F32).astype(BF16)


def _na_kernel(ws_ref, var_ref, q_ref, k_ref, v_ref, kc_ref, vc_ref, bias_ref, o_ref, *, band):
    i = pl.program_id(1)
    start = pl.multiple_of(ws_ref[i] * GRID_W, GRID_W)
    var = var_ref[i]
    kw = k_ref[pl.ds(start, band), :]
    vw = v_ref[pl.ds(start, band), :]
    kc, vc = kc_ref[...], vc_ref[...]
    q = q_ref[...]
    dn = (((1,), (1,)), ((), ()))
    head_of_lane = lax.broadcasted_iota(jnp.int32, q.shape, 1) // HEAD_DIM
    out = jnp.zeros(q.shape, F32)
    for h in range(C_HEADS):
        mine = head_of_lane == h
        qh = jnp.where(mine, q, jnp.zeros_like(q))
        s1 = lax.dot_general(qh, kw, dn, preferred_element_type=F32) + bias_ref[var, h].astype(F32)
        s2 = lax.dot_general(qh, kc, dn, preferred_element_type=F32)
        m = jnp.maximum(jnp.max(s1, axis=-1, keepdims=True), jnp.max(s2, axis=-1, keepdims=True))
        p1 = jnp.exp((s1 - m).astype(BF16))
        p2 = jnp.exp((s2 - m).astype(BF16))
        l = jnp.sum(p1.astype(F32), axis=-1, keepdims=True) + jnp.sum(p2.astype(F32), axis=-1, keepdims=True)
        o = jnp.dot(p1, vw, preferred_element_type=F32) + jnp.dot(p2, vc, preferred_element_type=F32)
        out = jnp.where(mine, o * (1.0 / l), out)
    o_ref[...] = out.astype(BF16)


def _na_plan(seq):
    rows = seq // GRID_W
    wr = min(NA_WIN_R, rows)
    wc = min(NA_WIN_C, GRID_W)
    qr = min(NA_Q_ROWS, rows)
    band_rows = min(rows, 2 * ((qr + wr) // 2))
    nblk = rows // qr
    n_row_off, n_col_off = 2 * NA_WIN_R - 1, 2 * NA_WIN_C - 1
    c = np.arange(GRID_W)
    cs = np.clip(c - wc // 2, 0, GRID_W - wc)
    col_ok = (c[None, :] >= cs[:, None]) & (c[None, :] < cs[:, None] + wc)
    col_idx = np.where(col_ok, c[None, :] - c[:, None] + (NA_WIN_C - 1), n_col_off).astype(np.int32)
    ws_list, var_list, variants = [], [], []
    for blk in range(nblk):
        r = blk * qr + np.arange(qr)
        rs = np.clip(r - wr // 2, 0, rows - wr)
        ws = int(np.clip(rs[0], 0, rows - band_rows))
        kr = ws + np.arange(band_rows)
        row_ok = (kr[None, :] >= rs[:, None]) & (kr[None, :] < rs[:, None] + wr)
        assert row_ok.sum(axis=1).min() == wr
        idx = np.where(row_ok, kr[None, :] - r[:, None] + (NA_WIN_R - 1), n_row_off).astype(np.int32)
        for v, known in enumerate(variants):
            if np.array_equal(known, idx):
                break
        else:
            v = len(variants)
            variants.append(idx)
        ws_list.append(ws)
        var_list.append(v)
    return (np.asarray(ws_list, np.int32), np.asarray(var_list, np.int32),
            np.stack(variants), col_idx, qr, band_rows)


def _natten(qn, kn, vn, kc, vc, rel_bias_l, batch):
    rows, w = qn.shape
    seq = rows // batch
    ctx_len = kc.shape[0] // batch
    ws, var, row_idx, col_idx, qr, band_rows = _na_plan(seq)
    nblk = ws.shape[0]
    tq, band = qr * GRID_W, band_rows * GRID_W
    nvar = row_idx.shape[0]
    padded = jnp.pad(rel_bias_l, ((0, 0), (0, 1), (0, 1)), constant_values=NEG_BIG)
    tiles = padded[:, :, col_idx].astype(BF16)
    bias = jnp.take(tiles, jnp.asarray(row_idx.reshape(-1)), axis=1)
    bias = bias.reshape(C_HEADS, nvar, qr, band_rows, GRID_W, GRID_W)
    bias = jnp.transpose(bias, (1, 0, 2, 4, 3, 5)).reshape(nvar, C_HEADS, tq, band)
    grid_spec = pltpu.PrefetchScalarGridSpec(
        num_scalar_prefetch=2,
        grid=(batch, nblk),
        in_specs=[
            pl.BlockSpec((tq, w), lambda b, i, ws_r, var_r: (b * nblk + i, 0)),
            pl.BlockSpec((seq, w), lambda b, i, ws_r, var_r: (b, 0)),
            pl.BlockSpec((seq, w), lambda b, i, ws_r, var_r: (b, 0)),
            pl.BlockSpec((ctx_len, w), lambda b, i, ws_r, var_r: (b, 0)),
            pl.BlockSpec((ctx_len, w), lambda b, i, ws_r, var_r: (b, 0)),
            pl.BlockSpec((nvar, C_HEADS, tq, band), lambda b, i, ws_r, var_r: (0, 0, 0, 0)),
        ],
        out_specs=pl.BlockSpec((tq, w), lambda b, i, ws_r, var_r: (b * nblk + i, 0)),
    )
    nbytes = 2 * nvar * C_HEADS * tq * band * 2 + 4 * seq * w * 2 + 6 * tq * (band + ctx_len) * 4
    return pl.pallas_call(
        functools.partial(_na_kernel, band=band),
        grid_spec=grid_spec,
        out_shape=jax.ShapeDtypeStruct((rows, w), BF16),
        compiler_params=_params(("parallel", "arbitrary"), nbytes),
        name="natten",
    )(jnp.asarray(ws), jnp.asarray(var), qn, kn, vn, kc, vc, bias)


def _merge_kernel(oa_ref, ob_ref, oc_ref, x_ref, gate_ref, shift_ref, scale_ref, ga_ref, gb_ref, gc_ref,
                  w_ref, gf_ref, wa_ref, wb_ref, xo_ref, h_ref, lg_ref, *, c_feature_major):
    def normed(o_ref, g_ref):
        return _rms(o_ref[...].astype(F32), g_ref[...]).astype(BF16)

    def normed_t(o_ref, g_ref):
        o = o_ref[...].astype(F32)
        return (o * lax.rsqrt(jnp.mean(o * o, axis=0, keepdims=True) + EPS) * g_ref[...]).astype(BF16)

    dn_t = (((0,), (0,)), ((), ()))
    a_w, b_w = oa_ref.shape[0], ob_ref.shape[1]
    y = lax.dot_general(normed_t(oa_ref, ga_ref), w_ref[0:a_w, :], dn_t, preferred_element_type=F32)
    y = y + jnp.dot(normed(ob_ref, gb_ref), w_ref[a_w:a_w + b_w, :], preferred_element_type=F32)
    if c_feature_major:
        y = y + lax.dot_general(normed_t(oc_ref, gc_ref), w_ref[a_w + b_w:, :], dn_t,
                                preferred_element_type=F32)
    else:
        y = y + jnp.dot(normed(oc_ref, gc_ref), w_ref[a_w + b_w:, :], preferred_element_type=F32)
    x = x_ref[...] + gate_ref[...] * y
    xo_ref[...] = x
    h = _rms(x, gf_ref[...]) * (1.0 + scale_ref[...]) + shift_ref[...]
    _store_row_tiled(h_ref, h)
    h_hi = h.astype(BF16)
    h_lo = (h - h_hi.astype(F32)).astype(BF16)
    t = jnp.dot(h_hi, wa_ref[...], preferred_element_type=F32) + jnp.dot(h_lo, wb_ref[...], preferred_element_type=F32)
    t = jnp.transpose(t)
    lg_ref[...] = t[0:N_EXPERTS] + t[N_EXPERTS:2 * N_EXPERTS]


def _router_split(w_router_l):
    d, n_exp = w_router_l.shape
    w_hi = w_router_l.astype(BF16)
    w_lo = (w_router_l - w_hi.astype(F32)).astype(BF16)
    wa = jnp.concatenate([w_hi, w_lo, jnp.zeros((d, LANES - 2 * n_exp), BF16)], axis=1)
    wb = jnp.concatenate([w_hi, jnp.zeros((d, LANES - n_exp), BF16)], axis=1)
    return wa, wb


def _merge(oa_t, ob, oc, c_feature_major, x2, mod, mod_row, g_a, g_b, g_c, w_out_bf, g_ffn, w_router_ab, tm):
    rows, d = x2.shape
    a_w, b_w = oa_t.shape[0], ob.shape[-1]
    nblk = ob.shape[2]
    assert ob.shape[3] == tm or nblk == 1

    def ob_index(i):
        tile = i % (2 * nblk)
        return (i // (2 * nblk), tile // nblk, jnp.where(tile < nblk, tile, 2 * nblk - 1 - tile), 0, 0)

    if nblk == 1 and ob.shape[3] != tm:
        ob = ob.reshape(-1, b_w)
        ob_spec = pl.BlockSpec((tm, b_w), lambda i: (i, 0))
    else:
        ob_spec = pl.BlockSpec((None, None, None, tm, b_w), ob_index)
    row = lambda i: (i, 0)
    col = lambda i: (0, i)
    const = lambda i: (0, 0)
    mod_spec = lambda j: pl.BlockSpec((None, None, 1, d), lambda i: (mod_row(i), j, 0, 0))
    if c_feature_major:
        c_w = oc.shape[0]
        oc_spec, gc_spec, g_c2 = pl.BlockSpec((c_w, tm), col), pl.BlockSpec((c_w, 1), const), g_c.reshape(-1, 1)
    else:
        c_w = oc.shape[1]
        oc_spec, gc_spec, g_c2 = pl.BlockSpec((tm, c_w), row), pl.BlockSpec((1, c_w), const), g_c.reshape(1, -1)
    return pl.pallas_call(
        functools.partial(_merge_kernel, c_feature_major=c_feature_major),
        grid=(rows // tm,),
        in_specs=[
            pl.BlockSpec((a_w, tm), col), ob_spec, oc_spec,
            pl.BlockSpec((tm, d), row),
            mod_spec(2), mod_spec(3), mod_spec(4),
            pl.BlockSpec((a_w, 1), const), pl.BlockSpec((1, b_w), const), gc_spec,
            pl.BlockSpec((a_w + b_w + c_w, d), const),
            pl.BlockSpec((1, d), const),
            pl.BlockSpec((d, LANES), const),
            pl.BlockSpec((d, LANES), const),
        ],
        out_specs=[pl.BlockSpec((tm, d), row),
                   pl.BlockSpec(_row_tiled_shape(tm, d), row),
                   pl.BlockSpec((N_EXPERTS, tm), lambda i: (0, i))],
        out_shape=[jax.ShapeDtypeStruct((rows, d), F32), jax.ShapeDtypeStruct(_row_tiled_shape(rows, d), F32),
                   jax.ShapeDtypeStruct((N_EXPERTS, rows), F32)],
        compiler_params=_params(("parallel",), 2 * d * d * 2 + 8 * tm * d * 4),
        name="merge_out",
    )(oa_t, ob, oc, x2, mod, mod, mod, g_a.reshape(-1, 1), g_b.reshape(1, -1), g_c2,
      w_out_bf, g_ffn.reshape(1, d), *w_router_ab)


def _route_kernel(lg_ref, tri_ref, idx_ref, gate_ref, *, n, cap):
    lg = lg_ref[...]
    e = jnp.exp(lg - jnp.max(lg, axis=0, keepdims=True))
    aff = e / jnp.sum(e, axis=0, keepdims=True)
    fcap = float(cap)

    def enough(v):
        return jnp.sum(jnp.where(aff >= v, 1.0, 0.0), axis=1, keepdims=True) >= fcap

    p = jnp.full((N_EXPERTS, 1), 2.0, F32)
    for j in (64, 32, 16, 8, 4, 2, 1):
        cand = p * (2.0 ** -j)
        p = jnp.where(enough(cand), p, cand)
    base = p * 0.5
    mant = jnp.zeros((N_EXPERTS, 1), F32)
    for j in range(22, -1, -1):
        cand = mant + float(2 ** j)
        mant = jnp.where(enough(base * (1.0 + cand * (2.0 ** -23))), cand, mant)
    thr = base * (1.0 + mant * (2.0 ** -23))

    tri = tri_ref[...]

    def excl_cumsum(mask):
        out, off = [], jnp.zeros((N_EXPERTS, 1), F32)
        for c in range(n // LANES):
            mc = mask[:, c * LANES:(c + 1) * LANES]
            inc = jnp.dot(mc.astype(BF16), tri, preferred_element_type=F32)
            out.append(inc - mc + off)
            off = off + inc[:, LANES - 1:LANES]
        return jnp.concatenate(out, axis=1), off

    gt = jnp.where(aff > thr, 1.0, 0.0)
    eq = jnp.where(aff == thr, 1.0, 0.0)
    need = fcap - jnp.sum(gt, axis=1, keepdims=True)
    eq_rank, _ = excl_cumsum(eq)
    sel = jnp.maximum(gt, jnp.where(eq_rank < need, eq, 0.0))
    pos, _ = excl_cumsum(sel)

    chosen = sel > 0.0
    tok = lax.broadcasted_iota(jnp.int32, (N_EXPERTS, n), 1)
    dist = jnp.where(chosen, tok - pos.astype(jnp.int32), 0)
    tok1 = jnp.where(chosen, tok + 1, 0)
    gate = jnp.where(chosen, aff, 0.0)
    for bit in range((n - 1).bit_length()):
        left = n - (1 << bit)
        dist_in, tok_in, gate_in = pltpu.roll(dist, left, 1), pltpu.roll(tok1, left, 1), pltpu.roll(gate, left, 1)
        arrives = ((dist_in >> bit) & 1) == 1
        stays = ((dist >> bit) & 1) == 0
        tok1 = jnp.where(arrives, tok_in, jnp.where(stays, tok1, 0))
        gate = jnp.where(arrives, gate_in, jnp.where(stays, gate, 0.0))
        dist = jnp.where(arrives, dist_in, jnp.where(stays, dist, 0))
    idx_ref[...] = tok1[:, :cap] - 1
    gate_ref[...] = gate[:, :cap]


def _route(logits_t, batch, cap):
    n = logits_t.shape[1] // batch
    assert n & (n - 1) == 0 and n % LANES == 0
    tri = jnp.asarray(np.triu(np.ones((LANES, LANES))), dtype=BF16)
    return pl.pallas_call(
        functools.partial(_route_kernel, n=n, cap=cap),
        grid=(batch,),
        in_specs=[pl.BlockSpec((N_EXPERTS, n), lambda b: (0, b)),
                  pl.BlockSpec((LANES, LANES), lambda b: (0, 0))],
        out_specs=[pl.BlockSpec((None, N_EXPERTS, cap), lambda b: (b, 0, 0)),
                   pl.BlockSpec((None, N_EXPERTS, cap), lambda b: (b, 0, 0))],
        out_shape=[jax.ShapeDtypeStruct((batch, N_EXPERTS, cap), jnp.int32),
                   jax.ShapeDtypeStruct((batch, N_EXPERTS, cap), F32)],
        compiler_params=_params(("parallel",), 64 * N_EXPERTS * n * 4),
        name="route_topk",
    )(logits_t, tri)


def _gather_kernel(idx_ref, h_ref, xe_ref, rows_scr, *, cap, chunks):
    b, ex = pl.program_id(0), pl.program_id(1)
    base = (b * N_EXPERTS + ex) * cap
    unroll = 2 * SUBLANES

    def body(i, carry):
        rows = [h_ref[_token_tile(idx_ref[base + i * unroll + u], chunks), :] for u in range(unroll)]
        for u in range(unroll):
            rows_scr[_token_tile(i * unroll + u, chunks), :] = rows[u]
        return carry

    lax.fori_loop(0, cap // unroll, body, 0)
    xe_ref[...] = _load_row_tiled(rows_scr, chunks).astype(BF16)


def _gather(idx, h3, d, batch, cap):
    chunks = d // LANES
    n = h3.shape[0] // chunks // batch
    grid_spec = pltpu.PrefetchScalarGridSpec(
        num_scalar_prefetch=1,
        grid=(batch, N_EXPERTS),
        in_specs=[pl.BlockSpec(_row_tiled_shape(n, d), lambda b, ex, idx_r: (b, 0))],
        out_specs=pl.BlockSpec((None, cap, d), lambda b, ex, idx_r: (ex, b, 0)),
        scratch_shapes=[pltpu.VMEM(_row_tiled_shape(cap, d), F32)],
    )
    return pl.pallas_call(
        functools.partial(_gather_kernel, cap=cap, chunks=chunks),
        grid_spec=grid_spec,
        out_shape=jax.ShapeDtypeStruct((N_EXPERTS, batch * cap, d), BF16),
        compiler_params=_params(("arbitrary", "arbitrary"), 2 * n * d * 4 + 3 * cap * d * 4),
        name="moe_gather",
    )(idx.reshape(-1), h3)


def _ffn_kernel(x_ref, wg_ref, wu_ref, wd_ref, y_ref, *, f_chunk):
    x = x_ref[...]
    y = None
    for c in range(wg_ref.shape[1] // f_chunk):
        cs = slice(c * f_chunk, (c + 1) * f_chunk)
        g = jnp.dot(x, wg_ref[:, cs], preferred_element_type=F32)
        u = jnp.dot(x, wu_ref[:, cs], preferred_element_type=F32)
        a = (g / (1.0 + jnp.exp(-g)) * u).astype(BF16)
        part = jnp.dot(a, wd_ref[cs, :], preferred_element_type=F32)
        y = part if y is None else y + part
    _store_row_tiled(y_ref, y)


def _expert_ffn(xe, layer, wg_bf, wu_bf, wd_bf, tm):
    n_exp, rows, d = xe.shape
    ff = wg_bf.shape[3]
    tm = min(tm, rows)
    return pl.pallas_call(
        functools.partial(_ffn_kernel, f_chunk=min(FFN_F_CHUNK, ff)),
        grid=(n_exp, rows // tm),
        in_specs=[
            pl.BlockSpec((None, tm, d), lambda ex, i: (ex, i, 0)),
            pl.BlockSpec((None, None, d, ff), lambda ex, i: (layer, ex, 0, 0)),
            pl.BlockSpec((None, None, d, ff), lambda ex, i: (layer, ex, 0, 0)),
            pl.BlockSpec((None, None, ff, d), lambda ex, i: (layer, ex, 0, 0)),
        ],
        out_specs=pl.BlockSpec((None,) + _row_tiled_shape(tm, d), lambda ex, i: (ex, i, 0)),
        out_shape=jax.ShapeDtypeStruct((n_exp,) + _row_tiled_shape(rows, d), F32),
        compiler_params=_params(("parallel", "arbitrary"), 2 * 3 * d * ff * 2 + 8 * tm * d * 4),
        name="expert_ffn",
    )(xe, wg_bf, wu_bf, wd_bf)


def _combine_kernel(idx_ref, gate_ref, y_ref, acc_ref, *, cap, chunks):
    b, ex = pl.program_id(0), pl.program_id(1)
    base = (b * N_EXPERTS + ex) * cap

    @pl.when(ex == 0)
    def _():
        acc_ref[...] = jnp.zeros_like(acc_ref)

    unroll = SUBLANES

    def body(i, carry):
        js = [i * unroll + u for u in range(unroll)]
        rs = [_token_tile(idx_ref[base + j], chunks) for j in js]
        new = [acc_ref[r, :] + y_ref[_token_tile(j, chunks), :] * gate_ref[base + j] for r, j in zip(rs, js)]
        for r, v in zip(rs, new):
            acc_ref[r, :] = v
        return carry

    lax.fori_loop(0, cap // unroll, body, 0)


def _combine(idx, gate, y3, d, batch, n, cap):
    chunks = d // LANES
    grid_spec = pltpu.PrefetchScalarGridSpec(
        num_scalar_prefetch=2,
        grid=(batch, N_EXPERTS),
        in_specs=[pl.BlockSpec((None,) + _row_tiled_shape(cap, d), lambda b, ex, idx_r, gate_r: (ex, b, 0))],
        out_specs=pl.BlockSpec(_row_tiled_shape(n, d), lambda b, ex, idx_r, gate_r: (b, 0)),
    )
    return pl.pallas_call(
        functools.partial(_combine_kernel, cap=cap, chunks=chunks),
        grid_spec=grid_spec,
        out_shape=jax.ShapeDtypeStruct(_row_tiled_shape(batch * n, d), F32),
        compiler_params=_params(("arbitrary", "arbitrary"), 2 * n * d * 4 + 2 * cap * d * 4),
        name="moe_combine",
    )(idx.reshape(-1), gate.reshape(-1), y3)


def _resid_kernel(x_ref, m_ref, gate_ref, gf_ref, o_ref, *, final):
    x = x_ref[...] + gate_ref[...] * _load_row_tiled(m_ref, x_ref.shape[1] // LANES)
    o_ref[...] = _rms(x, gf_ref[...]) if final else x


def _residual(x2, moe3, mod, mod_row, g_final, final, tm):
    rows, d = x2.shape
    row = lambda i: (i, 0)
    return pl.pallas_call(
        functools.partial(_resid_kernel, final=final),
        grid=(rows // tm,),
        in_specs=[pl.BlockSpec((tm, d), row), pl.BlockSpec(_row_tiled_shape(tm, d), row),
                  pl.BlockSpec((None, None, 1, d), lambda i: (mod_row(i), 5, 0, 0)),
                  pl.BlockSpec((1, d), lambda i: (0, 0))],
        out_specs=pl.BlockSpec((tm, d), row),
        out_shape=jax.ShapeDtypeStruct((rows, d), F32),
        compiler_params=_params(("parallel",), 6 * tm * d * 4),
        name="moe_residual_final" if final else "moe_residual",
    )(x2, moe3, mod, g_final.reshape(1, d))


def _moe(h3, logits_t, batch, layer, wg_bf, wu_bf, wd_bf):
    d = wg_bf.shape[2]
    n = logits_t.shape[1] // batch
    cap = max(1, CAPACITY_FACTOR * n // N_EXPERTS)
    idx, gate = _route(logits_t, batch, cap)
    xe = _gather(idx, h3, d, batch, cap)
    y3 = _expert_ffn(xe, layer, wg_bf, wu_bf, wd_bf, FFN_ROW_TILE)
    return _combine(idx, gate, y3, d, batch, n, cap)


def _rope_tables(n):
    t = np.arange(n)
    n_freq = HEAD_DIM // 4
    inv = ROPE_THETA ** (-np.arange(n_freq, dtype=np.float64) / n_freq)
    ang_r = (t // GRID_W)[:, None] * inv[None, :]
    ang_c = (t % GRID_W)[:, None] * inv[None, :]
    ang = np.concatenate([ang_r, ang_r, ang_c, ang_c], axis=1)
    sign = np.concatenate([-np.ones(n_freq), np.ones(n_freq)] * 2)[None, :]
    reps = LANES // HEAD_DIM
    cos = np.tile(np.cos(ang), (1, reps))
    sin = np.tile(np.sin(ang) * sign, (1, reps))
    return jnp.asarray(cos, F32), jnp.asarray(sin, F32)


def kernel(x, c, ctx, c_ctx, w_ada, b_ada, g_mix, g_ffn, w_in, g_q, g_k, rel_bias, g_out_a, g_out_b, g_out_c, w_out, w_router, w_gate, w_up, w_down, g_final):
    batch, seq, d = x.shape
    ctx_len = ctx.shape[1]
    depth = w_ada.shape[0]
    b_w = g_out_b.shape[1]
    group_dim = b_w // B_GROUPS
    ctx_row = batch
    assert batch < MOD_ROWS and seq % ROW_TILE == 0

    cvec = jnp.zeros((MOD_ROWS, d), F32).at[:batch].set(c).at[ctx_row].set(c_ctx)
    mod_all = _ada(cvec, w_ada, b_ada).reshape(depth, MOD_ROWS, 6, 1, d)

    rope_tabs = _rope_tables(seq)
    blk = np.arange(LANES) // HEAD_DIM
    ones_hd = jnp.asarray(blk[:, None] == blk[None, :], dtype=BF16)
    cs = _channel_dft(b_w, group_dim)
    dft_lat = _dft_tables(seq, group_dim)
    dft_ctx = _dft_tables(ctx_len, group_dim)

    lat_tile = ROW_TILE
    ctx_tile = min(ROW_TILE, ctx_len)
    lat_row = lambda i: i // (seq // lat_tile)
    ctx_mod_row = lambda i: ctx_row
    group_a = A_Q_HEADS // A_KV_HEADS

    n_exp, ff = w_gate.shape[1], w_gate.shape[3]
    w_stacks = (w_gate.reshape(-1, ff), w_up.reshape(-1, ff), w_down.reshape(-1, d))
    xl = x.reshape(batch * seq, d)
    xc = ctx.reshape(batch * ctx_len, d)
    for l in range(depth):
        last = l == depth - 1
        mod = mod_all[l]
        w_in_bf = w_in[l].astype(BF16)
        w_out_bf = w_out[l].astype(BF16)
        w_router_t = _router_split(w_router[l])
        gq_t = jnp.tile(g_q[l], LANES // HEAD_DIM).reshape(1, LANES)
        gk_t = jnp.tile(g_k[l], LANES // HEAD_DIM).reshape(1, LANES)

        qa, ka, va_t, yc, ys, qn, kn, vn = _inproj(xl, mod, lat_row, g_mix[l], w_in_bf, gq_t, gk_t, ones_hd, cs,
                                                   rope_tabs, seq, lat_tile)
        qa_c, ka_c, va_c_t, yc_c, ys_c, qn_c, kn_c, vn_c, vn_c_t = _inproj(
            xc, mod, ctx_mod_row, g_mix[l], w_in_bf, gq_t, gk_t, ones_hd, cs, None, ctx_len, ctx_tile)
        o_a_t, wg_bf, wu_bf, wd_bf = _attention(qa, ka, va_t, ka_c, va_c_t, batch, A_KV_HEADS, group_a, ATTN_Q_TILE,
                                                cast=[(w, l, depth) for w in w_stacks])
        wg_bf, wu_bf = wg_bf.reshape(1, n_exp, d, ff), wu_bf.reshape(1, n_exp, d, ff)
        wd_bf = wd_bf.reshape(1, n_exp, ff, d)
        o_b = _fourier(dft_lat[0], dft_lat[1], yc, ys, batch, min(FOURIER_ROW_TILE, seq // 2))
        o_c = _natten(qn, kn, vn, kn_c, vn_c, rel_bias[l], batch)
        x_mid, h2, logits_t = _merge(o_a_t, o_b, o_c, False, xl, mod, lat_row, g_out_a[l], g_out_b[l],
                                     g_out_c[l], w_out_bf, g_ffn[l], w_router_t, lat_tile)
        moe = _moe(h2, logits_t, batch, 0, wg_bf, wu_bf, wd_bf)
        xl = _residual(x_mid, moe, mod, lat_row, g_final, last, lat_tile)
        if not last:
            o_a_c_t = _attention(qa_c, ka_c, va_c_t, None, None, batch, A_KV_HEADS, group_a, ctx_len)
            o_b_c = _fourier(dft_ctx[0], dft_ctx[1], yc_c, ys_c, batch, ctx_len // 2)
            o_c_c_t = _attention(qn_c, kn_c, vn_c_t, None, None, batch, C_HEADS, 1, ctx_len)
            xc_mid, h2_c, logits_c = _merge(o_a_c_t, o_b_c, o_c_c_t, True, xc, mod, ctx_mod_row, g_out_a[l],
                                            g_out_b[l], g_out_c[l], w_out_bf, g_ffn[l], w_router_t, ctx_tile)
            moe_c = _moe(h2_c, logits_c, batch, 0, wg_bf, wu_bf, wd_bf)
            xc = _residual(xc_mid, moe_c, mod, ctx_mod_row, g_final, False, ctx_tile)
    return xl.reshape(batch, seq, d)
```

```python
import functools
import math

import numpy as np
import jax
import jax.numpy as jnp
from jax import lax
from jax.experimental import pallas as pl
from jax.experimental.pallas import tpu as pltpu

F32 = jnp.float32
BF16 = jnp.bfloat16
HIGHEST = lax.Precision.HIGHEST

GRID_W = 64
HEAD_DIM = 64
A_Q_HEADS = 8
A_KV_HEADS = 2
B_GROUPS = 4
C_HEADS = 4
NA_WIN_R = 8
NA_WIN_C = 16
ROPE_THETA = 10000.0
N_EXPERTS = 16
CAPACITY_FACTOR = 2
EPS = 1e-6
MOD_ROWS = 16

LANES = 128
SUBLANES = 8
VMEM_BYTES_V7X = 64 * 1024 * 1024

ROW_TILE = 512
ATTN_Q_TILE = 256
ATTN_KEY_CHUNK = 512
ATTN_SCORE_SLOTS = 2
FOURIER_ROW_TILE = 512
NA_Q_ROWS = 4
FFN_ROW_TILE = 512
FFN_F_CHUNK = 512
COMBINE_ROWS_PER_STEP = 512
NEG_BIG = -1e30


def _vmem_limit(nbytes):
    return int(min(max(2 * nbytes, 32 * 1024 * 1024), VMEM_BYTES_V7X - 8 * 1024 * 1024))


def _params(sem, nbytes):
    return pltpu.CompilerParams(dimension_semantics=sem, vmem_limit_bytes=_vmem_limit(nbytes))


def _rms(x, g):
    return x * lax.rsqrt(jnp.mean(x * x, axis=-1, keepdims=True) + EPS) * g


def _row_tiled_shape(rows, d):
    assert d % LANES == 0
    return (rows * (d // LANES), LANES)


def _store_row_tiled(ref, value):
    rows, d = value.shape
    chunks = d // LANES
    for c in range(chunks):
        ref[pl.ds(c, rows, stride=chunks), :] = value[:, c * LANES:(c + 1) * LANES]


def _load_row_tiled(ref, chunks):
    rows = ref.shape[0] // chunks
    return jnp.concatenate([ref[pl.ds(c, rows, stride=chunks), :] for c in range(chunks)], axis=1)


def _token_tile(i, chunks):
    return pl.ds(pl.multiple_of(i * chunks, chunks), chunks)


def _ada_kernel(c_ref, w_ref, b_ref, o_ref):
    c = c_ref[...]
    sc = c / (1.0 + jnp.exp(-c))
    o_ref[...] = jnp.dot(sc, w_ref[...], precision=HIGHEST, preferred_element_type=F32) + b_ref[...]


def _ada(cvec, w_ada, b_ada):
    depth, d, n6 = w_ada.shape
    tn = 1024
    return pl.pallas_call(
        _ada_kernel,
        grid=(depth, n6 // tn),
        in_specs=[
            pl.BlockSpec((MOD_ROWS, d), lambda l, j: (0, 0)),
            pl.BlockSpec((None, d, tn), lambda l, j: (l, 0, j)),
            pl.BlockSpec((None, 1, tn), lambda l, j: (l, 0, j)),
        ],
        out_specs=pl.BlockSpec((None, MOD_ROWS, tn), lambda l, j: (l, 0, j)),
        out_shape=jax.ShapeDtypeStruct((depth, MOD_ROWS, n6), F32),
        compiler_params=_params(("arbitrary", "arbitrary"), 2 * d * tn * 4),
        name="ada_mod",
    )(cvec, w_ada, b_ada.reshape(depth, 1, n6))


def _head_norm(t, ones, g):
    ssq = jnp.dot((t * t).astype(BF16), ones, preferred_element_type=F32)
    return t * lax.rsqrt(ssq * (1.0 / HEAD_DIM) + EPS) * g


def _rope(t, cos, sin):
    lane = lax.broadcasted_iota(jnp.int32, t.shape, 1)
    first = (lane & 31) < 16
    partner = jnp.where(first, pltpu.roll(t, LANES - 16, 1), pltpu.roll(t, 16, 1))
    return t * cos + partner * sin


def _inproj_kernel(*refs, rope, widths):
    a_w, kv_w, b_w, c_w = widths
    if rope:
        (x_ref, g_ref, shift_ref, scale_ref, w_ref, gq_ref, gk_ref, ones_ref, cs_ref, cos_ref, sin_ref,
         qa_ref, ka_ref, va_ref, yc_ref, ys_ref, qn_ref, kn_ref, vn_ref) = refs
        cos, sin = cos_ref[...], sin_ref[...]
    else:
        (x_ref, g_ref, shift_ref, scale_ref, w_ref, gq_ref, gk_ref, ones_ref, cs_ref,
         qa_ref, ka_ref, va_ref, yc_ref, ys_ref, qn_ref, kn_ref, vn_ref, vnt_ref) = refs
        cos = sin = None
    h = _rms(x_ref[...], g_ref[...]) * (1.0 + scale_ref[...]) + shift_ref[...]
    p = jnp.dot(h.astype(BF16), w_ref[...], preferred_element_type=F32)
    ones = ones_ref[...]
    q_scale = HEAD_DIM ** -0.5
    for j in range(a_w // LANES):
        t = _head_norm(p[:, j * LANES:(j + 1) * LANES], ones, gq_ref[...])
        if rope:
            t = _rope(t, cos, sin)
        qa_ref[:, j * LANES:(j + 1) * LANES] = (t * q_scale).astype(BF16)
    o = a_w
    t = _head_norm(p[:, o:o + kv_w], ones, gk_ref[...])
    if rope:
        t = _rope(t, cos, sin)
    ka_ref[...] = t.astype(BF16)
    o += kv_w
    va_ref[...] = jnp.transpose(p[:, o:o + kv_w]).astype(BF16)
    o += kv_w
    y = jnp.dot(p[:, o:o + b_w].astype(BF16), cs_ref[...], preferred_element_type=F32)
    yc_ref[...] = y[:, :b_w].astype(BF16)
    ys_ref[...] = y[:, b_w:].astype(BF16)
    o += b_w
    qn_ref[...] = (p[:, o:o + c_w] * q_scale).astype(BF16)
    o += c_w
    kn_ref[...] = p[:, o:o + c_w].astype(BF16)
    o += c_w
    vn_ref[...] = p[:, o:o + c_w].astype(BF16)
    if not rope:
        vnt_ref[...] = jnp.transpose(p[:, o:o + c_w]).astype(BF16)


def _inproj(x2, mod, mod_row, g_mix, w_in_bf, gq_t, gk_t, ones_hd, cs, rope_tabs, seq, tm):
    rows, d = x2.shape
    in_w = w_in_bf.shape[1]
    a_w = A_Q_HEADS * HEAD_DIM
    kv_w = A_KV_HEADS * HEAD_DIM
    b_w = cs.shape[0]
    c_w = C_HEADS * HEAD_DIM
    assert kv_w == LANES and in_w == a_w + 2 * kv_w + b_w + 3 * c_w
    rope = rope_tabs is not None
    steps_per_seq = seq // tm
    const = lambda i: (0, 0)
    row = lambda i: (i, 0)
    in_specs = [
        pl.BlockSpec((tm, d), row),
        pl.BlockSpec((1, d), const),
        pl.BlockSpec((None, None, 1, d), lambda i: (mod_row(i), 0, 0, 0)),
        pl.BlockSpec((None, None, 1, d), lambda i: (mod_row(i), 1, 0, 0)),
        pl.BlockSpec((d, in_w), const),
        pl.BlockSpec((1, LANES), const),
        pl.BlockSpec((1, LANES), const),
        pl.BlockSpec((LANES, LANES), const),
        pl.BlockSpec((b_w, 2 * b_w), const),
    ]
    args = [x2, g_mix.reshape(1, d), mod, mod, w_in_bf, gq_t, gk_t, ones_hd, cs]
    if rope:
        in_specs += [pl.BlockSpec((tm, LANES), lambda i: (i % steps_per_seq, 0))] * 2
        args += list(rope_tabs)
    out_w = [a_w, kv_w, kv_w, b_w, b_w, c_w, c_w, c_w]
    out_specs = [pl.BlockSpec((tm, w), row) for w in out_w]
    out_shape = [jax.ShapeDtypeStruct((rows, w), BF16) for w in out_w]
    out_specs[2] = pl.BlockSpec((kv_w, tm), lambda i: (0, i))
    out_shape[2] = jax.ShapeDtypeStruct((kv_w, rows), BF16)
    if not rope:
        out_specs.append(pl.BlockSpec((c_w, tm), lambda i: (0, i)))
        out_shape.append(jax.ShapeDtypeStruct((c_w, rows), BF16))
    return pl.pallas_call(
        functools.partial(_inproj_kernel, rope=rope, widths=(a_w, kv_w, b_w, c_w)),
        grid=(rows // tm,),
        in_specs=in_specs,
        out_specs=out_specs,
        out_shape=out_shape,
        compiler_params=_params(("parallel",), 2 * d * in_w * 2 + 2 * tm * d * 4 + 3 * tm * in_w * 4),
        name="inproj_rope" if rope else "inproj",
    )(*args)


def _place_head(q_ref, h, kv, kw):
    assert 2 * HEAD_DIM == LANES
    src_blk, src_off = divmod(h * HEAD_DIM, LANES)
    dst_blk, dst_off = divmod(kv * HEAD_DIM, LANES)
    blk = q_ref[:, src_blk * LANES:(src_blk + 1) * LANES].astype(F32)
    if src_off != dst_off:
        blk = pltpu.roll(blk, HEAD_DIM, 1)
    lane = lax.broadcasted_iota(jnp.int32, blk.shape, 1)
    keep = (lane >= dst_off) & (lane < dst_off + HEAD_DIM)
    blk = jnp.transpose(jnp.where(keep, blk, 0.0)).astype(BF16)
    parts = [blk if j == dst_blk else jnp.zeros_like(blk) for j in range(kw // LANES)]
    return parts[0] if len(parts) == 1 else jnp.concatenate(parts, axis=0)


def _attn_kernel(*refs, n_kv, group, two, n_cast):
    n_in = 5 if two else 3
    cast_in, cast_out = refs[n_in:n_in + n_cast], refs[n_in + n_cast + 1:n_in + 2 * n_cast + 1]
    o_ref, s_scr = refs[n_in + n_cast], refs[-1]
    if two:
        q_ref, k1_ref, v1_ref, k2_ref, v2_ref = refs[:n_in]
        segments = ((k1_ref, v1_ref), (k2_ref, v2_ref))
    else:
        q_ref, k1_ref, v1_ref = refs[:n_in]
        segments = ((k1_ref, v1_ref),)
    for src, dst in zip(cast_in, cast_out):
        dst[...] = src[...].astype(BF16)
    kw = k1_ref.shape[1]
    tq = q_ref.shape[0]
    dn = (((1,), (1,)), ((), ()))
    sum_rows = 2 * SUBLANES
    n_heads = n_kv * group
    chunks, base = [], 0
    for seg, (k_ref, _) in enumerate(segments):
        n_seg = k_ref.shape[0]
        for c0 in range(0, n_seg, ATTN_KEY_CHUNK):
            chunks.append((seg, c0, min(ATTN_KEY_CHUNK, n_seg - c0), base + c0))
        base += n_seg

    def scores(h):
        slot = h % s_scr.shape[0]
        qp = _place_head(q_ref, h, h // group, kw)
        m = None
        for seg, c0, size, row in chunks:
            s = jnp.dot(segments[seg][0][c0:c0 + size, :], qp, preferred_element_type=F32)
            s_scr[slot, row:row + size, :] = s
            mc = jnp.max(s, axis=0, keepdims=True)
            m = mc if m is None else jnp.maximum(m, mc)
        return m

    def finish(h, m):
        slot, kv = h % s_scr.shape[0], h // group
        vs = slice(kv * HEAD_DIM, (kv + 1) * HEAD_DIM)
        acc = jnp.zeros((HEAD_DIM + sum_rows, tq), F32)
        for seg, c0, size, row in chunks:
            va = jnp.concatenate([segments[seg][1][vs, c0:c0 + size], jnp.ones((sum_rows, size), BF16)], axis=0)
            p = jnp.exp((s_scr[slot, row:row + size, :] - m).astype(BF16))
            acc = acc + jnp.dot(va, p, preferred_element_type=F32)
        o = acc[:HEAD_DIM] * (1.0 / acc[HEAD_DIM:HEAD_DIM + 1])
        o_ref[h * HEAD_DIM:(h + 1) * HEAD_DIM, :] = o.astype(BF16)

    n_slots = s_scr.shape[0]
    maxes = [scores(h) for h in range(min(n_slots - 1, n_heads))]
    for h in range(n_heads):
        if h + n_slots - 1 < n_heads:
            maxes.append(scores(h + n_slots - 1))
        finish(h, maxes[h])


def _attention(q, k1, v1t, k2, v2t, batch, n_kv, group, tq, cast=()):
    rows, qw = q.shape
    n_q = rows // batch
    n1 = k1.shape[0] // batch
    kw = k1.shape[1]
    two = k2 is not None
    steps = n_q // tq
    in_specs = [
        pl.BlockSpec((tq, qw), lambda b, i: (b * steps + i, 0)),
        pl.BlockSpec((n1, kw), lambda b, i: (b, 0)),
        pl.BlockSpec((kw, n1), lambda b, i: (0, b)),
    ]
    args = [q, k1, v1t]
    n_keys = n1
    if two:
        n2 = k2.shape[0] // batch
        in_specs += [pl.BlockSpec((n2, kw), lambda b, i: (b, 0)), pl.BlockSpec((kw, n2), lambda b, i: (0, b))]
        args += [k2, v2t]
        n_keys += n2
    out_specs = [pl.BlockSpec((qw, tq), lambda b, i: (0, b * steps + i))]
    out_shape = [jax.ShapeDtypeStruct((qw, rows), BF16)]
    n_steps = batch * steps
    for src, part, parts in cast:
        part_rows = src.shape[0] // parts
        slab = part_rows // n_steps
        assert slab * n_steps * parts == src.shape[0] and slab % (2 * SUBLANES) == 0
        in_specs.append(pl.BlockSpec((slab, src.shape[1]),
                                     lambda b, i, part=part: (part * n_steps + b * steps + i, 0)))
        out_specs.append(pl.BlockSpec((slab, src.shape[1]), lambda b, i: (b * steps + i, 0)))
        out_shape.append(jax.ShapeDtypeStruct((part_rows, src.shape[1]), BF16))
        args.append(src)
    outs = pl.pallas_call(
        functools.partial(_attn_kernel, n_kv=n_kv, group=group, two=two, n_cast=len(cast)),
        grid=(batch, steps),
        in_specs=in_specs,
        out_specs=out_specs,
        out_shape=out_shape,
        scratch_shapes=[pltpu.VMEM((ATTN_SCORE_SLOTS, n_keys, tq), F32)],
        compiler_params=_params(("parallel", "arbitrary"), 4 * n_keys * kw * 2 + 4 * tq * n_keys * 4),
        name="attn_two_seg" if two else "attn_one_seg",
    )(*args)
    return outs if cast else outs[0]


def _fourier_kernel(c_ref, s_ref, cn_ref, sn_ref, yc_ref, ys_ref, rev_ref, o_ref):
    yc, ys = yc_ref[...], ys_ref[...]
    a = jnp.dot(c_ref[...], yc, preferred_element_type=F32)
    b = jnp.dot(s_ref[...], ys, preferred_element_type=F32)
    o_ref[0] = (a + b).astype(BF16)
    nxt = (jnp.dot(cn_ref[...], yc, preferred_element_type=F32)
           - jnp.dot(sn_ref[...], ys, preferred_element_type=F32))[0:1]
    mirrored = jnp.dot(rev_ref[...], (a - b).astype(BF16), preferred_element_type=F32)
    row = lax.broadcasted_iota(jnp.int32, mirrored.shape, 0)
    o_ref[1] = jnp.where(row == 0, nxt, mirrored).astype(BF16)


def _fourier(c_tab, s_tab, yc, ys, batch, tm):
    n = c_tab.shape[1]
    w = yc.shape[1]
    nblk = n // 2 // tm
    rev = np.zeros((tm, tm), np.float32)
    rev[np.arange(1, tm), tm - np.arange(1, tm)] = 1.0
    few = 2 * SUBLANES
    next_rows = tm // few
    return pl.pallas_call(
        _fourier_kernel,
        grid=(nblk, batch),
        in_specs=[
            pl.BlockSpec((tm, n), lambda i, b: (i, 0)),
            pl.BlockSpec((tm, n), lambda i, b: (i, 0)),
            pl.BlockSpec((few, n), lambda i, b: ((i + 1) * next_rows, 0)),
            pl.BlockSpec((few, n), lambda i, b: ((i + 1) * next_rows, 0)),
            pl.BlockSpec((n, w), lambda i, b: (b, 0)),
            pl.BlockSpec((n, w), lambda i, b: (b, 0)),
            pl.BlockSpec((tm, tm), lambda i, b: (0, 0)),
        ],
        out_specs=pl.BlockSpec((None, 2, None, tm, w), lambda i, b: (b, 0, i, 0, 0)),
        out_shape=jax.ShapeDtypeStruct((batch, 2, nblk, tm, w), BF16),
        compiler_params=_params(("arbitrary", "arbitrary"), 4 * tm * n * 2 + 4 * n * w * 2),
        name="fourier",
    )(c_tab, s_tab, c_tab, s_tab, yc, ys, jnp.asarray(rev, F32).astype(BF16))


def _dft_tables(n, group_dim):
    scale = 1.0 / math.sqrt(n * group_dim)
    rows = n // 2 + 2 * SUBLANES
    r = int(round(math.sqrt(n)))
    if r * r != n:
        k = jnp.arange(rows, dtype=jnp.int32)
        t = jnp.arange(n, dtype=jnp.int32)
        ang = ((k[:, None] * t[None, :]) % n).astype(F32) * (2.0 * math.pi / n)
        return (jnp.cos(ang) * scale).astype(BF16), (-jnp.sin(ang) * scale).astype(BF16)
    k = jnp.arange(rows, dtype=jnp.int32)[:, None]
    t = jnp.arange(r, dtype=jnp.int32)[None, :]
    ang_a = ((k * t) % r).astype(F32) * (2.0 * math.pi / r)
    ang_b = ((k * t) % n).astype(F32) * (2.0 * math.pi / n)
    ca, sa = jnp.cos(ang_a)[:, :, None], jnp.sin(ang_a)[:, :, None]
    cb, sb = jnp.cos(ang_b)[:, None, :], jnp.sin(ang_b)[:, None, :]
    c = ((ca * cb - sa * sb) * scale).astype(BF16)
    s = ((sa * cb + ca * sb) * (-scale)).astype(BF16)
    return c.reshape(rows, n), s.reshape(rows, n)


def _channel_dft(width, group_dim):
    j = np.arange(group_dim)
    ang = 2.0 * np.pi * ((j[:, None] * j[None, :]) % group_dim) / group_dim
    eye = np.eye(width // group_dim)
    cs = np.concatenate([np.kron(eye, np.cos(ang)), np.kron(eye, np.sin(ang))], axis=1)
    return jnp.asarray(cs, dtype=F32).astype(BF16)


def _na_kernel(ws_ref, var_ref, q_ref, k_ref, v_ref, kc_ref, vc_ref, bias_ref, o_ref, *, band):
    i = pl.program_id(1)
    start = pl.multiple_of(ws_ref[i] * GRID_W, GRID_W)
    var = var_ref[i]
    kw = k_ref[pl.ds(start, band), :]
    vw = v_ref[pl.ds(start, band), :]
    kc, vc = kc_ref[...], vc_ref[...]
    q = q_ref[...]
    dn = (((1,), (1,)), ((), ()))
    head_of_lane = lax.broadcasted_iota(jnp.int32, q.shape, 1) // HEAD_DIM
    out = jnp.zeros(q.shape, F32)
    for h in range(C_HEADS):
        mine = head_of_lane == h
        qh = jnp.where(mine, q, jnp.zeros_like(q))
        s1 = lax.dot_general(qh, kw, dn, preferred_element_type=F32) + bias_ref[var, h].astype(F32)
        s2 = lax.dot_general(qh, kc, dn, preferred_element_type=F32)
        m = jnp.maximum(jnp.max(s1, axis=-1, keepdims=True), jnp.max(s2, axis=-1, keepdims=True))
        p1 = jnp.exp((s1 - m).astype(BF16))
        p2 = jnp.exp((s2 - m).astype(BF16))
        l = jnp.sum(p1.astype(F32), axis=-1, keepdims=True) + jnp.sum(p2.astype(F32), axis=-1, keepdims=True)
        o = jnp.dot(p1, vw, preferred_element_type=F32) + jnp.dot(p2, vc, preferred_element_type=F32)
        out = jnp.where(mine, o * (1.0 / l), out)
    o_ref[...] = out.astype(BF16)


def _na_plan(seq):
    rows = seq // GRID_W
    wr = min(NA_WIN_R, rows)
    wc = min(NA_WIN_C, GRID_W)
    qr = min(NA_Q_ROWS, rows)
    band_rows = min(rows, 2 * ((qr + wr) // 2))
    nblk = rows // qr
    n_row_off, n_col_off = 2 * NA_WIN_R - 1, 2 * NA_WIN_C - 1
    c = np.arange(GRID_W)
    cs = np.clip(c - wc // 2, 0, GRID_W - wc)
    col_ok = (c[None, :] >= cs[:, None]) & (c[None, :] < cs[:, None] + wc)
    col_idx = np.where(col_ok, c[None, :] - c[:, None] + (NA_WIN_C - 1), n_col_off).astype(np.int32)
    ws_list, var_list, variants = [], [], []
    for blk in range(nblk):
        r = blk * qr + np.arange(qr)
        rs = np.clip(r - wr // 2, 0, rows - wr)
        ws = int(np.clip(rs[0], 0, rows - band_rows))
        kr = ws + np.arange(band_rows)
        row_ok = (kr[None, :] >= rs[:, None]) & (kr[None, :] < rs[:, None] + wr)
        assert row_ok.sum(axis=1).min() == wr
        idx = np.where(row_ok, kr[None, :] - r[:, None] + (NA_WIN_R - 1), n_row_off).astype(np.int32)
        for v, known in enumerate(variants):
            if np.array_equal(known, idx):
                break
        else:
            v = len(variants)
            variants.append(idx)
        ws_list.append(ws)
        var_list.append(v)
    return (np.asarray(ws_list, np.int32), np.asarray(var_list, np.int32),
            np.stack(variants), col_idx, qr, band_rows)


def _natten(qn, kn, vn, kc, vc, rel_bias_l, batch):
    rows, w = qn.shape
    seq = rows // batch
    ctx_len = kc.shape[0] // batch
    ws, var, row_idx, col_idx, qr, band_rows = _na_plan(seq)
    nblk = ws.shape[0]
    tq, band = qr * GRID_W, band_rows * GRID_W
    nvar = row_idx.shape[0]
    padded = jnp.pad(rel_bias_l, ((0, 0), (0, 1), (0, 1)), constant_values=NEG_BIG)
    tiles = padded[:, :, col_idx].astype(BF16)
    bias = jnp.take(tiles, jnp.asarray(row_idx.reshape(-1)), axis=1)
    bias = bias.reshape(C_HEADS, nvar, qr, band_rows, GRID_W, GRID_W)
    bias = jnp.transpose(bias, (1, 0, 2, 4, 3, 5)).reshape(nvar, C_HEADS, tq, band)
    grid_spec = pltpu.PrefetchScalarGridSpec(
        num_scalar_prefetch=2,
        grid=(batch, nblk),
        in_specs=[
            pl.BlockSpec((tq, w), lambda b, i, ws_r, var_r: (b * nblk + i, 0)),
            pl.BlockSpec((seq, w), lambda b, i, ws_r, var_r: (b, 0)),
            pl.BlockSpec((seq, w), lambda b, i, ws_r, var_r: (b, 0)),
            pl.BlockSpec((ctx_len, w), lambda b, i, ws_r, var_r: (b, 0)),
            pl.BlockSpec((ctx_len, w), lambda b, i, ws_r, var_r: (b, 0)),
            pl.BlockSpec((nvar, C_HEADS, tq, band), lambda b, i, ws_r, var_r: (0, 0, 0, 0)),
        ],
        out_specs=pl.BlockSpec((tq, w), lambda b, i, ws_r, var_r: (b * nblk + i, 0)),
    )
    nbytes = 2 * nvar * C_HEADS * tq * band * 2 + 4 * seq * w * 2 + 6 * tq * (band + ctx_len) * 4
    return pl.pallas_call(
        functools.partial(_na_kernel, band=band),
        grid_spec=grid_spec,
        out_shape=jax.ShapeDtypeStruct((rows, w), BF16),
        compiler_params=_params(("parallel", "arbitrary"), nbytes),
        name="natten",
    )(jnp.asarray(ws), jnp.asarray(var), qn, kn, vn, kc, vc, bias)


def _merge_kernel(oa_ref, ob_ref, oc_ref, x_ref, gate_ref, shift_ref, scale_ref, ga_ref, gb_ref, gc_ref,
                  w_ref, gf_ref, wa_ref, wb_ref, xo_ref, h_ref, lg_ref, *, c_feature_major):
    def normed(o_ref, g_ref):
        return _rms(o_ref[...].astype(F32), g_ref[...]).astype(BF16)

    def normed_t(o_ref, g_ref):
        o = o_ref[...].astype(F32)
        return (o * lax.rsqrt(jnp.mean(o * o, axis=0, keepdims=True) + EPS) * g_ref[...]).astype(BF16)

    dn_t = (((0,), (0,)), ((), ()))
    a_w, b_w = oa_ref.shape[0], ob_ref.shape[1]
    y = lax.dot_general(normed_t(oa_ref, ga_ref), w_ref[0:a_w, :], dn_t, preferred_element_type=F32)
    y = y + jnp.dot(normed(ob_ref, gb_ref), w_ref[a_w:a_w + b_w, :], preferred_element_type=F32)
    if c_feature_major:
        y = y + lax.dot_general(normed_t(oc_ref, gc_ref), w_ref[a_w + b_w:, :], dn_t,
                                preferred_element_type=F32)
    else:
        y = y + jnp.dot(normed(oc_ref, gc_ref), w_ref[a_w + b_w:, :], preferred_element_type=F32)
    x = x_ref[...] + gate_ref[...] * y
    xo_ref[...] = x
    h = _rms(x, gf_ref[...]) * (1.0 + scale_ref[...]) + shift_ref[...]
    _store_row_tiled(h_ref, h)
    h_hi = h.astype(BF16)
    h_lo = (h - h_hi.astype(F32)).astype(BF16)
    t = jnp.dot(h_hi, wa_ref[...], preferred_element_type=F32) + jnp.dot(h_lo, wb_ref[...], preferred_element_type=F32)
    t = jnp.transpose(t)
    lg_ref[...] = t[0:N_EXPERTS] + t[N_EXPERTS:2 * N_EXPERTS]


def _router_split(w_router_l):
    d, n_exp = w_router_l.shape
    w_hi = w_router_l.astype(BF16)
    w_lo = (w_router_l - w_hi.astype(F32)).astype(BF16)
    wa = jnp.concatenate([w_hi, w_lo, jnp.zeros((d, LANES - 2 * n_exp), BF16)], axis=1)
    wb = jnp.concatenate([w_hi, jnp.zeros((d, LANES - n_exp), BF16)], axis=1)
    return wa, wb


def _merge(oa_t, ob, oc, c_feature_major, x2, mod, mod_row, g_a, g_b, g_c, w_out_bf, g_ffn, w_router_ab, tm):
    rows, d = x2.shape
    a_w, b_w = oa_t.shape[0], ob.shape[-1]
    nblk = ob.shape[2]
    assert ob.shape[3] == tm or nblk == 1

    def ob_index(i):
        tile = i % (2 * nblk)
        return (i // (2 * nblk), tile // nblk, jnp.where(tile < nblk, tile, 2 * nblk - 1 - tile), 0, 0)

    if nblk == 1 and ob.shape[3] != tm:
        ob = ob.reshape(-1, b_w)
        ob_spec = pl.BlockSpec((tm, b_w), lambda i: (i, 0))
    else:
        ob_spec = pl.BlockSpec((None, None, None, tm, b_w), ob_index)
    row = lambda i: (i, 0)
    col = lambda i: (0, i)
    const = lambda i: (0, 0)
    mod_spec = lambda j: pl.BlockSpec((None, None, 1, d), lambda i: (mod_row(i), j, 0, 0))
    if c_feature_major:
        c_w = oc.shape[0]
        oc_spec, gc_spec, g_c2 = pl.BlockSpec((c_w, tm), col), pl.BlockSpec((c_w, 1), const), g_c.reshape(-1, 1)
    else:
        c_w = oc.shape[1]
        oc_spec, gc_spec, g_c2 = pl.BlockSpec((tm, c_w), row), pl.BlockSpec((1, c_w), const), g_c.reshape(1, -1)
    return pl.pallas_call(
        functools.partial(_merge_kernel, c_feature_major=c_feature_major),
        grid=(rows // tm,),
        in_specs=[
            pl.BlockSpec((a_w, tm), col), ob_spec, oc_spec,
            pl.BlockSpec((tm, d), row),
            mod_spec(2), mod_spec(3), mod_spec(4),
            pl.BlockSpec((a_w, 1), const), pl.BlockSpec((1, b_w), const), gc_spec,
            pl.BlockSpec((a_w + b_w + c_w, d), const),
            pl.BlockSpec((1, d), const),
            pl.BlockSpec((d, LANES), const),
            pl.BlockSpec((d, LANES), const),
        ],
        out_specs=[pl.BlockSpec((tm, d), row),
                   pl.BlockSpec(_row_tiled_shape(tm, d), row),
                   pl.BlockSpec((N_EXPERTS, tm), lambda i: (0, i))],
        out_shape=[jax.ShapeDtypeStruct((rows, d), F32), jax.ShapeDtypeStruct(_row_tiled_shape(rows, d), F32),
                   jax.ShapeDtypeStruct((N_EXPERTS, rows), F32)],
        compiler_params=_params(("parallel",), 2 * d * d * 2 + 8 * tm * d * 4),
        name="merge_out",
    )(oa_t, ob, oc, x2, mod, mod, mod, g_a.reshape(-1, 1), g_b.reshape(1, -1), g_c2,
      w_out_bf, g_ffn.reshape(1, d), *w_router_ab)


def _route_kernel(lg_ref, tri_ref, idx_ref, gate_ref, *, n, cap):
    lg = lg_ref[...]
    e = jnp.exp(lg - jnp.max(lg, axis=0, keepdims=True))
    aff = e / jnp.sum(e, axis=0, keepdims=True)
    fcap = float(cap)

    def enough(v):
        return jnp.sum(jnp.where(aff >= v, 1.0, 0.0), axis=1, keepdims=True) >= fcap

    p = jnp.full((N_EXPERTS, 1), 2.0, F32)
    for j in (64, 32, 16, 8, 4, 2, 1):
        cand = p * (2.0 ** -j)
        p = jnp.where(enough(cand), p, cand)
    base = p * 0.5
    mant = jnp.zeros((N_EXPERTS, 1), F32)
    for j in range(22, -1, -1):
        cand = mant + float(2 ** j)
        mant = jnp.where(enough(base * (1.0 + cand * (2.0 ** -23))), cand, mant)
    thr = base * (1.0 + mant * (2.0 ** -23))

    tri = tri_ref[...]

    def excl_cumsum(mask):
        out, off = [], jnp.zeros((N_EXPERTS, 1), F32)
        for c in range(n // LANES):
            mc = mask[:, c * LANES:(c + 1) * LANES]
            inc = jnp.dot(mc.astype(BF16), tri, preferred_element_type=F32)
            out.append(inc - mc + off)
            off = off + inc[:, LANES - 1:LANES]
        return jnp.concatenate(out, axis=1), off

    gt = jnp.where(aff > thr, 1.0, 0.0)
    eq = jnp.where(aff == thr, 1.0, 0.0)
    need = fcap - jnp.sum(gt, axis=1, keepdims=True)
    eq_rank, _ = excl_cumsum(eq)
    sel = jnp.maximum(gt, jnp.where(eq_rank < need, eq, 0.0))
    pos, _ = excl_cumsum(sel)

    chosen = sel > 0.0
    tok = lax.broadcasted_iota(jnp.int32, (N_EXPERTS, n), 1)
    dist = jnp.where(chosen, tok - pos.astype(jnp.int32), 0)
    tok1 = jnp.where(chosen, tok + 1, 0)
    gate = jnp.where(chosen, aff, 0.0)
    for bit in range((n - 1).bit_length()):
        left = n - (1 << bit)
        dist_in, tok_in, gate_in = pltpu.roll(dist, left, 1), pltpu.roll(tok1, left, 1), pltpu.roll(gate, left, 1)
        arrives = ((dist_in >> bit) & 1) == 1
        stays = ((dist >> bit) & 1) == 0
        tok1 = jnp.where(arrives, tok_in, jnp.where(stays, tok1, 0))
        gate = jnp.where(arrives, gate_in, jnp.where(stays, gate, 0.0))
        dist = jnp.where(arrives, dist_in, jnp.where(stays, dist, 0))
    idx_ref[...] = tok1[:, :cap] - 1
    gate_ref[...] = gate[:, :cap]


def _route(logits_t, batch, cap):
    n = logits_t.shape[1] // batch
    assert n & (n - 1) == 0 and n % LANES == 0
    tri = jnp.asarray(np.triu(np.ones((LANES, LANES))), dtype=BF16)
    return pl.pallas_call(
        functools.partial(_route_kernel, n=n, cap=cap),
        grid=(batch,),
        in_specs=[pl.BlockSpec((N_EXPERTS, n), lambda b: (0, b)),
                  pl.BlockSpec((LANES, LANES), lambda b: (0, 0))],
        out_specs=[pl.BlockSpec((None, N_EXPERTS, cap), lambda b: (b, 0, 0)),
                   pl.BlockSpec((None, N_EXPERTS, cap), lambda b: (b, 0, 0))],
        out_shape=[jax.ShapeDtypeStruct((batch, N_EXPERTS, cap), jnp.int32),
                   jax.ShapeDtypeStruct((batch, N_EXPERTS, cap), F32)],
        compiler_params=_params(("parallel",), 64 * N_EXPERTS * n * 4),
        name="route_topk",
    )(logits_t, tri)


def _gather_kernel(idx_ref, h_ref, xe_ref, rows_scr, *, cap, chunks):
    b, ex = pl.program_id(0), pl.program_id(1)
    base = (b * N_EXPERTS + ex) * cap
    unroll = 2 * SUBLANES

    def body(i, carry):
        rows = [h_ref[_token_tile(idx_ref[base + i * unroll + u], chunks), :] for u in range(unroll)]
        for u in range(unroll):
            rows_scr[_token_tile(i * unroll + u, chunks), :] = rows[u]
        return carry

    lax.fori_loop(0, cap // unroll, body, 0)
    xe_ref[...] = _load_row_tiled(rows_scr, chunks).astype(BF16)


def _gather(idx, h3, d, batch, cap):
    chunks = d // LANES
    n = h3.shape[0] // chunks // batch
    grid_spec = pltpu.PrefetchScalarGridSpec(
        num_scalar_prefetch=1,
        grid=(batch, N_EXPERTS),
        in_specs=[pl.BlockSpec(_row_tiled_shape(n, d), lambda b, ex, idx_r: (b, 0))],
        out_specs=pl.BlockSpec((None, cap, d), lambda b, ex, idx_r: (ex, b, 0)),
        scratch_shapes=[pltpu.VMEM(_row_tiled_shape(cap, d), F32)],
    )
    return pl.pallas_call(
        functools.partial(_gather_kernel, cap=cap, chunks=chunks),
        grid_spec=grid_spec,
        out_shape=jax.ShapeDtypeStruct((N_EXPERTS, batch * cap, d), BF16),
        compiler_params=_params(("arbitrary", "arbitrary"), 2 * n * d * 4 + 3 * cap * d * 4),
        name="moe_gather",
    )(idx.reshape(-1), h3)


def _ffn_kernel(x_ref, wg_ref, wu_ref, wd_ref, y_ref, *, f_chunk):
    x = x_ref[...]
    y = None
    for c in range(wg_ref.shape[1] // f_chunk):
        cs = slice(c * f_chunk, (c + 1) * f_chunk)
        g = jnp.dot(x, wg_ref[:, cs], preferred_element_type=F32)
        u = jnp.dot(x, wu_ref[:, cs], preferred_element_type=F32)
        a = (g / (1.0 + jnp.exp(-g)) * u).astype(BF16)
        part = jnp.dot(a, wd_ref[cs, :], preferred_element_type=F32)
        y = part if y is None else y + part
    _store_row_tiled(y_ref, y)


def _expert_ffn(xe, layer, wg_bf, wu_bf, wd_bf, tm):
    n_exp, rows, d = xe.shape
    ff = wg_bf.shape[3]
    tm = min(tm, rows)
    return pl.pallas_call(
        functools.partial(_ffn_kernel, f_chunk=min(FFN_F_CHUNK, ff)),
        grid=(n_exp, rows // tm),
        in_specs=[
            pl.BlockSpec((None, tm, d), lambda ex, i: (ex, i, 0)),
            pl.BlockSpec((None, None, d, ff), lambda ex, i: (layer, ex, 0, 0)),
            pl.BlockSpec((None, None, d, ff), lambda ex, i: (layer, ex, 0, 0)),
            pl.BlockSpec((None, None, ff, d), lambda ex, i: (layer, ex, 0, 0)),
        ],
        out_specs=pl.BlockSpec((None,) + _row_tiled_shape(tm, d), lambda ex, i: (ex, i, 0)),
        out_shape=jax.ShapeDtypeStruct((n_exp,) + _row_tiled_shape(rows, d), F32),
        compiler_params=_params(("parallel", "arbitrary"), 2 * 3 * d * ff * 2 + 8 * tm * d * 4),
        name="expert_ffn",
    )(xe, wg_bf, wu_bf, wd_bf)


def _combine_kernel(idx_ref, gate_ref, y_ref, acc_ref, *, cap, chunks, experts_per_step):
    b, step = pl.program_id(0), pl.program_id(1)

    @pl.when(step == 0)
    def _():
        acc_ref[...] = jnp.zeros_like(acc_ref)

    unroll = SUBLANES
    for e in range(experts_per_step):
        base = (b * N_EXPERTS + step * experts_per_step + e) * cap

        def body(i, carry, e=e, base=base):
            js = [i * unroll + u for u in range(unroll)]
            rs = [_token_tile(idx_ref[base + j], chunks) for j in js]
            new = [acc_ref[r, :] + y_ref[e, _token_tile(j, chunks), :] * gate_ref[base + j]
                   for r, j in zip(rs, js)]
            for r, v in zip(rs, new):
                acc_ref[r, :] = v
            return carry

        lax.fori_loop(0, cap // unroll, body, 0)


def _combine(idx, gate, y3, d, batch, n, cap):
    chunks = d // LANES
    experts_per_step = N_EXPERTS if cap * N_EXPERTS <= COMBINE_ROWS_PER_STEP else 1
    grid_spec = pltpu.PrefetchScalarGridSpec(
        num_scalar_prefetch=2,
        grid=(batch, N_EXPERTS // experts_per_step),
        in_specs=[pl.BlockSpec((experts_per_step,) + _row_tiled_shape(cap, d),
                               lambda b, s, idx_r, gate_r: (s, b, 0))],
        out_specs=pl.BlockSpec(_row_tiled_shape(n, d), lambda b, s, idx_r, gate_r: (b, 0)),
    )
    return pl.pallas_call(
        functools.partial(_combine_kernel, cap=cap, chunks=chunks, experts_per_step=experts_per_step),
        grid_spec=grid_spec,
        out_shape=jax.ShapeDtypeStruct(_row_tiled_shape(batch * n, d), F32),
        compiler_params=_params(("arbitrary", "arbitrary"), 2 * n * d * 4 + 2 * cap * d * 4),
        name="moe_combine",
    )(idx.reshape(-1), gate.reshape(-1), y3)


def _resid_kernel(x_ref, m_ref, gate_ref, gf_ref, o_ref, *, final):
    x = x_ref[...] + gate_ref[...] * _load_row_tiled(m_ref, x_ref.shape[1] // LANES)
    o_ref[...] = _rms(x, gf_ref[...]) if final else x


def _residual(x2, moe3, mod, mod_row, g_final, final, tm):
    rows, d = x2.shape
    row = lambda i: (i, 0)
    return pl.pallas_call(
        functools.partial(_resid_kernel, final=final),
        grid=(rows // tm,),
        in_specs=[pl.BlockSpec((tm, d), row), pl.BlockSpec(_row_tiled_shape(tm, d), row),
                  pl.BlockSpec((None, None, 1, d), lambda i: (mod_row(i), 5, 0, 0)),
                  pl.BlockSpec((1, d), lambda i: (0, 0))],
        out_specs=pl.BlockSpec((tm, d), row),
        out_shape=jax.ShapeDtypeStruct((rows, d), F32),
        compiler_params=_params(("parallel",), 6 * tm * d * 4),
        name="moe_residual_final" if final else "moe_residual",
    )(x2, moe3, mod, g_final.reshape(1, d))


def _moe(h3, logits_t, batch, layer, wg_bf, wu_bf, wd_bf):
    d = wg_bf.shape[2]
    n = logits_t.shape[1] // batch
    cap = max(1, CAPACITY_FACTOR * n // N_EXPERTS)
    idx, gate = _route(logits_t, batch, cap)
    xe = _gather(idx, h3, d, batch, cap)
    y3 = _expert_ffn(xe, layer, wg_bf, wu_bf, wd_bf, FFN_ROW_TILE)
    return _combine(idx, gate, y3, d, batch, n, cap)


def _rope_tables(n):
    t = np.arange(n)
    n_freq = HEAD_DIM // 4
    inv = ROPE_THETA ** (-np.arange(n_freq, dtype=np.float64) / n_freq)
    ang_r = (t // GRID_W)[:, None] * inv[None, :]
    ang_c = (t % GRID_W)[:, None] * inv[None, :]
    ang = np.concatenate([ang_r, ang_r, ang_c, ang_c], axis=1)
    sign = np.concatenate([-np.ones(n_freq), np.ones(n_freq)] * 2)[None, :]
    reps = LANES // HEAD_DIM
    cos = np.tile(np.cos(ang), (1, reps))
    sin = np.tile(np.sin(ang) * sign, (1, reps))
    return jnp.asarray(cos, F32), jnp.asarray(sin, F32)


def kernel(x, c, ctx, c_ctx, w_ada, b_ada, g_mix, g_ffn, w_in, g_q, g_k, rel_bias, g_out_a, g_out_b, g_out_c, w_out, w_router, w_gate, w_up, w_down, g_final):
    batch, seq, d = x.shape
    ctx_len = ctx.shape[1]
    depth = w_ada.shape[0]
    b_w = g_out_b.shape[1]
    group_dim = b_w // B_GROUPS
    ctx_row = batch
    assert batch < MOD_ROWS and seq % ROW_TILE == 0

    cvec = jnp.zeros((MOD_ROWS, d), F32).at[:batch].set(c).at[ctx_row].set(c_ctx)
    mod_all = _ada(cvec, w_ada, b_ada).reshape(depth, MOD_ROWS, 6, 1, d)

    rope_tabs = _rope_tables(seq)
    blk = np.arange(LANES) // HEAD_DIM
    ones_hd = jnp.asarray(blk[:, None] == blk[None, :], dtype=BF16)
    cs = _channel_dft(b_w, group_dim)
    dft_lat = _dft_tables(seq, group_dim)
    dft_ctx = _dft_tables(ctx_len, group_dim)

    lat_tile = ROW_TILE
    ctx_tile = min(ROW_TILE, ctx_len)
    lat_row = lambda i: i // (seq // lat_tile)
    ctx_mod_row = lambda i: ctx_row
    group_a = A_Q_HEADS // A_KV_HEADS

    n_exp, ff = w_gate.shape[1], w_gate.shape[3]
    w_stacks = (w_gate.reshape(-1, ff), w_up.reshape(-1, ff), w_down.reshape(-1, d))
    xl = x.reshape(batch * seq, d)
    xc = ctx.reshape(batch * ctx_len, d)
    for l in range(depth):
        last = l == depth - 1
        mod = mod_all[l]
        w_in_bf = w_in[l].astype(BF16)
        w_out_bf = w_out[l].astype(BF16)
        w_router_t = _router_split(w_router[l])
        gq_t = jnp.tile(g_q[l], LANES // HEAD_DIM).reshape(1, LANES)
        gk_t = jnp.tile(g_k[l], LANES // HEAD_DIM).reshape(1, LANES)

        qa, ka, va_t, yc, ys, qn, kn, vn = _inproj(xl, mod, lat_row, g_mix[l], w_in_bf, gq_t, gk_t, ones_hd, cs,
                                                   rope_tabs, seq, lat_tile)
        qa_c, ka_c, va_c_t, yc_c, ys_c, qn_c, kn_c, vn_c, vn_c_t = _inproj(
            xc, mod, ctx_mod_row, g_mix[l], w_in_bf, gq_t, gk_t, ones_hd, cs, None, ctx_len, ctx_tile)
        o_a_t, wg_bf, wu_bf, wd_bf = _attention(qa, ka, va_t, ka_c, va_c_t, batch, A_KV_HEADS, group_a, ATTN_Q_TILE,
                                                cast=[(w, l, depth) for w in w_stacks])
        wg_bf, wu_bf = wg_bf.reshape(1, n_exp, d, ff), wu_bf.reshape(1, n_exp, d, ff)
        wd_bf = wd_bf.reshape(1, n_exp, ff, d)
        o_b = _fourier(dft_lat[0], dft_lat[1], yc, ys, batch, min(FOURIER_ROW_TILE, seq // 2))
        o_c = _natten(qn, kn, vn, kn_c, vn_c, rel_bias[l], batch)
        x_mid, h2, logits_t = _merge(o_a_t, o_b, o_c, False, xl, mod, lat_row, g_out_a[l], g_out_b[l],
                                     g_out_c[l], w_out_bf, g_ffn[l], w_router_t, lat_tile)
        moe = _moe(h2, logits_t, batch, 0, wg_bf, wu_bf, wd_bf)
        xl = _residual(x_mid, moe, mod, lat_row, g_final, last, lat_tile)
        if not last:
            o_a_c_t = _attention(qa_c, ka_c, va_c_t, None, None, batch, A_KV_HEADS, group_a, ctx_len)
            o_b_c = _fourier(dft_ctx[0], dft_ctx[1], yc_c, ys_c, batch, ctx_len // 2)
            o_c_c_t = _attention(qn_c, kn_c, vn_c_t, None, None, batch, C_HEADS, 1, ctx_len)
            xc_mid, h2_c, logits_c = _merge(o_a_c_t, o_b_c, o_c_c_t, True, xc, mod, ctx_mod_row, g_out_a[l],
                                            g_out_b[l], g_out_c[l], w_out_bf, g_ffn[l], w_router_t, ctx_tile)
            moe_c = _moe(h2_c, logits_c, batch, 0, wg_bf, wu_bf, wd_bf)
            xc = _residual(xc_mid, moe_c, mod, ctx_mod_row, g_final, False, ctx_tile)
    return xl.reshape(batch, seq, d)
```

```python
import functools
import math

import numpy as np
import jax
import jax.numpy as jnp
from jax import lax
from jax.experimental import pallas as pl
from jax.experimental.pallas import tpu as pltpu

F32 = jnp.float32
BF16 = jnp.bfloat16
HIGHEST = lax.Precision.HIGHEST

GRID_W = 64
HEAD_DIM = 64
A_Q_HEADS = 8
A_KV_HEADS = 2
B_GROUPS = 4
C_HEADS = 4
NA_WIN_R = 8
NA_WIN_C = 16
ROPE_THETA = 10000.0
N_EXPERTS = 16
CAPACITY_FACTOR = 2
EPS = 1e-6
MOD_ROWS = 16

LANES = 128
SUBLANES = 8
VMEM_BYTES_V7X = 64 * 1024 * 1024

ROW_TILE = 512
ATTN_Q_TILE = 256
ATTN_KEY_CHUNK = 512
ATTN_SCORE_SLOTS = 2
FOURIER_ROW_TILE = 512
NA_Q_ROWS = 4
FFN_ROW_TILE = 512
FFN_F_CHUNK = 512
COMBINE_ROWS_PER_STEP = 512
NEG_BIG = -1e30


def _vmem_limit(nbytes):
    return int(min(max(2 * nbytes, 32 * 1024 * 1024), VMEM_BYTES_V7X - 8 * 1024 * 1024))


def _params(sem, nbytes):
    return pltpu.CompilerParams(dimension_semantics=sem, vmem_limit_bytes=_vmem_limit(nbytes))


def _rms(x, g):
    return x * lax.rsqrt(jnp.mean(x * x, axis=-1, keepdims=True) + EPS) * g


def _row_tiled_shape(rows, d):
    assert d % LANES == 0
    return (rows * (d // LANES), LANES)


def _store_row_tiled(ref, value):
    rows, d = value.shape
    chunks = d // LANES
    for c in range(chunks):
        ref[pl.ds(c, rows, stride=chunks), :] = value[:, c * LANES:(c + 1) * LANES]


def _load_row_tiled(ref, chunks):
    rows = ref.shape[0] // chunks
    return jnp.concatenate([ref[pl.ds(c, rows, stride=chunks), :] for c in range(chunks)], axis=1)


def _token_tile(i, chunks):
    return pl.ds(pl.multiple_of(i * chunks, chunks), chunks)


def _ada_kernel(c_ref, w_ref, b_ref, o_ref):
    c = c_ref[...]
    sc = c / (1.0 + jnp.exp(-c))
    o_ref[...] = jnp.dot(sc, w_ref[...], precision=HIGHEST, preferred_element_type=F32) + b_ref[...]


def _ada(cvec, w_ada, b_ada):
    depth, d, n6 = w_ada.shape
    tn = 1024
    return pl.pallas_call(
        _ada_kernel,
        grid=(depth, n6 // tn),
        in_specs=[
            pl.BlockSpec((MOD_ROWS, d), lambda l, j: (0, 0)),
            pl.BlockSpec((None, d, tn), lambda l, j: (l, 0, j)),
            pl.BlockSpec((None, 1, tn), lambda l, j: (l, 0, j)),
        ],
        out_specs=pl.BlockSpec((None, MOD_ROWS, tn), lambda l, j: (l, 0, j)),
        out_shape=jax.ShapeDtypeStruct((depth, MOD_ROWS, n6), F32),
        compiler_params=_params(("arbitrary", "arbitrary"), 2 * d * tn * 4),
        name="ada_mod",
    )(cvec, w_ada, b_ada.reshape(depth, 1, n6))


def _head_norm(t, ones, g):
    ssq = jnp.dot((t * t).astype(BF16), ones, preferred_element_type=F32)
    return t * lax.rsqrt(ssq * (1.0 / HEAD_DIM) + EPS) * g


def _rope(t, cos, sin):
    lane = lax.broadcasted_iota(jnp.int32, t.shape, 1)
    first = (lane & 31) < 16
    partner = jnp.where(first, pltpu.roll(t, LANES - 16, 1), pltpu.roll(t, 16, 1))
    return t * cos + partner * sin


def _inproj_kernel(*refs, rope, widths):
    a_w, kv_w, b_w, c_w = widths
    if rope:
        (x_ref, g_ref, shift_ref, scale_ref, w_ref, gq_ref, gk_ref, ones_ref, cs_ref, cos_ref, sin_ref,
         qa_ref, ka_ref, va_ref, yc_ref, ys_ref, qn_ref, kn_ref, vn_ref) = refs
        cos, sin = cos_ref[...], sin_ref[...]
    else:
        (x_ref, g_ref, shift_ref, scale_ref, w_ref, gq_ref, gk_ref, ones_ref, cs_ref,
         qa_ref, ka_ref, va_ref, yc_ref, ys_ref, qn_ref, kn_ref, vn_ref, vnt_ref) = refs
        cos = sin = None
    h = _rms(x_ref[...], g_ref[...]) * (1.0 + scale_ref[...]) + shift_ref[...]
    p = jnp.dot(h.astype(BF16), w_ref[...], preferred_element_type=F32)
    ones = ones_ref[...]
    q_scale = HEAD_DIM ** -0.5
    for j in range(a_w // LANES):
        t = _head_norm(p[:, j * LANES:(j + 1) * LANES], ones, gq_ref[...])
        if rope:
            t = _rope(t, cos, sin)
        qa_ref[:, j * LANES:(j + 1) * LANES] = (t * q_scale).astype(BF16)
    o = a_w
    t = _head_norm(p[:, o:o + kv_w], ones, gk_ref[...])
    if rope:
        t = _rope(t, cos, sin)
    ka_ref[...] = t.astype(BF16)
    o += kv_w
    va_ref[...] = jnp.transpose(p[:, o:o + kv_w]).astype(BF16)
    o += kv_w
    y = jnp.dot(p[:, o:o + b_w].astype(BF16), cs_ref[...], preferred_element_type=F32)
    yc_ref[...] = y[:, :b_w].astype(BF16)
    ys_ref[...] = y[:, b_w:].astype(BF16)
    o += b_w
    qn_ref[...] = (p[:, o:o + c_w] * q_scale).astype(BF16)
    o += c_w
    kn_ref[...] = p[:, o:o + c_w].astype(BF16)
    o += c_w
    vn_ref[...] = p[:, o:o + c_w].astype(BF16)
    if not rope:
        vnt_ref[...] = jnp.transpose(p[:, o:o + c_w]).astype(BF16)


def _inproj(x2, mod, mod_row, g_mix, w_in_bf, gq_t, gk_t, ones_hd, cs, rope_tabs, seq, tm):
    rows, d = x2.shape
    in_w = w_in_bf.shape[1]
    a_w = A_Q_HEADS * HEAD_DIM
    kv_w = A_KV_HEADS * HEAD_DIM
    b_w = cs.shape[0]
    c_w = C_HEADS * HEAD_DIM
    assert kv_w == LANES and in_w == a_w + 2 * kv_w + b_w + 3 * c_w
    rope = rope_tabs is not None
    steps_per_seq = seq // tm
    const = lambda i: (0, 0)
    row = lambda i: (i, 0)
    in_specs = [
        pl.BlockSpec((tm, d), row),
        pl.BlockSpec((1, d), const),
        pl.BlockSpec((None, None, 1, d), lambda i: (mod_row(i), 0, 0, 0)),
        pl.BlockSpec((None, None, 1, d), lambda i: (mod_row(i), 1, 0, 0)),
        pl.BlockSpec((d, in_w), const),
        pl.BlockSpec((1, LANES), const),
        pl.BlockSpec((1, LANES), const),
        pl.BlockSpec((LANES, LANES), const),
        pl.BlockSpec((b_w, 2 * b_w), const),
    ]
    args = [x2, g_mix.reshape(1, d), mod, mod, w_in_bf, gq_t, gk_t, ones_hd, cs]
    if rope:
        in_specs += [pl.BlockSpec((tm, LANES), lambda i: (i % steps_per_seq, 0))] * 2
        args += list(rope_tabs)
    out_w = [a_w, kv_w, kv_w, b_w, b_w, c_w, c_w, c_w]
    out_specs = [pl.BlockSpec((tm, w), row) for w in out_w]
    out_shape = [jax.ShapeDtypeStruct((rows, w), BF16) for w in out_w]
    out_specs[2] = pl.BlockSpec((kv_w, tm), lambda i: (0, i))
    out_shape[2] = jax.ShapeDtypeStruct((kv_w, rows), BF16)
    if not rope:
        out_specs.append(pl.BlockSpec((c_w, tm), lambda i: (0, i)))
        out_shape.append(jax.ShapeDtypeStruct((c_w, rows), BF16))
    return pl.pallas_call(
        functools.partial(_inproj_kernel, rope=rope, widths=(a_w, kv_w, b_w, c_w)),
        grid=(rows // tm,),
        in_specs=in_specs,
        out_specs=out_specs,
        out_shape=out_shape,
        compiler_params=_params(("parallel",), 2 * d * in_w * 2 + 2 * tm * d * 4 + 3 * tm * in_w * 4),
        name="inproj_rope" if rope else "inproj",
    )(*args)


def _place_head(q_ref, h, kv, kw):
    assert 2 * HEAD_DIM == LANES
    src_blk, src_off = divmod(h * HEAD_DIM, LANES)
    dst_blk, dst_off = divmod(kv * HEAD_DIM, LANES)
    blk = q_ref[:, src_blk * LANES:(src_blk + 1) * LANES].astype(F32)
    if src_off != dst_off:
        blk = pltpu.roll(blk, HEAD_DIM, 1)
    lane = lax.broadcasted_iota(jnp.int32, blk.shape, 1)
    keep = (lane >= dst_off) & (lane < dst_off + HEAD_DIM)
    blk = jnp.transpose(jnp.where(keep, blk, 0.0)).astype(BF16)
    parts = [blk if j == dst_blk else jnp.zeros_like(blk) for j in range(kw // LANES)]
    return parts[0] if len(parts) == 1 else jnp.concatenate(parts, axis=0)


def _attn_kernel(*refs, n_kv, group, two, n_cast):
    n_in = 5 if two else 3
    cast_in, cast_out = refs[n_in:n_in + n_cast], refs[n_in + n_cast + 1:n_in + 2 * n_cast + 1]
    o_ref, s_scr = refs[n_in + n_cast], refs[-1]
    if two:
        q_ref, k1_ref, v1_ref, k2_ref, v2_ref = refs[:n_in]
        segments = ((k1_ref, v1_ref), (k2_ref, v2_ref))
    else:
        q_ref, k1_ref, v1_ref = refs[:n_in]
        segments = ((k1_ref, v1_ref),)
    for src, dst in zip(cast_in, cast_out):
        dst[...] = src[...].astype(BF16)
    kw = k1_ref.shape[1]
    tq = q_ref.shape[0]
    dn = (((1,), (1,)), ((), ()))
    sum_rows = 2 * SUBLANES
    n_heads = n_kv * group
    chunks, base = [], 0
    for seg, (k_ref, _) in enumerate(segments):
        n_seg = k_ref.shape[0]
        for c0 in range(0, n_seg, ATTN_KEY_CHUNK):
            chunks.append((seg, c0, min(ATTN_KEY_CHUNK, n_seg - c0), base + c0))
        base += n_seg

    def scores(h):
        slot = h % s_scr.shape[0]
        qp = _place_head(q_ref, h, h // group, kw)
        m = None
        for seg, c0, size, row in chunks:
            s = jnp.dot(segments[seg][0][c0:c0 + size, :], qp, preferred_element_type=F32)
            s_scr[slot, row:row + size, :] = s.astype(BF16)
            mc = jnp.max(s, axis=0, keepdims=True)
            m = mc if m is None else jnp.maximum(m, mc)
        return m

    def finish(h, m):
        slot, kv = h % s_scr.shape[0], h // group
        vs = slice(kv * HEAD_DIM, (kv + 1) * HEAD_DIM)
        acc = jnp.zeros((HEAD_DIM + sum_rows, tq), F32)
        for seg, c0, size, row in chunks:
            va = jnp.concatenate([segments[seg][1][vs, c0:c0 + size], jnp.ones((sum_rows, size), BF16)], axis=0)
            p = jnp.exp(s_scr[slot, row:row + size, :] - m.astype(BF16))
            acc = acc + jnp.dot(va, p, preferred_element_type=F32)
        o = acc[:HEAD_DIM] * (1.0 / acc[HEAD_DIM:HEAD_DIM + 1])
        o_ref[h * HEAD_DIM:(h + 1) * HEAD_DIM, :] = o.astype(BF16)

    n_slots = s_scr.shape[0]
    maxes = [scores(h) for h in range(min(n_slots - 1, n_heads))]
    for h in range(n_heads):
        if h + n_slots - 1 < n_heads:
            maxes.append(scores(h + n_slots - 1))
        finish(h, maxes[h])


def _attention(q, k1, v1t, k2, v2t, batch, n_kv, group, tq, cast=()):
    rows, qw = q.shape
    n_q = rows // batch
    n1 = k1.shape[0] // batch
    kw = k1.shape[1]
    two = k2 is not None
    steps = n_q // tq
    in_specs = [
        pl.BlockSpec((tq, qw), lambda b, i: (b * steps + i, 0)),
        pl.BlockSpec((n1, kw), lambda b, i: (b, 0)),
        pl.BlockSpec((kw, n1), lambda b, i: (0, b)),
    ]
    args = [q, k1, v1t]
    n_keys = n1
    if two:
        n2 = k2.shape[0] // batch
        in_specs += [pl.BlockSpec((n2, kw), lambda b, i: (b, 0)), pl.BlockSpec((kw, n2), lambda b, i: (0, b))]
        args += [k2, v2t]
        n_keys += n2
    out_specs = [pl.BlockSpec((qw, tq), lambda b, i: (0, b * steps + i))]
    out_shape = [jax.ShapeDtypeStruct((qw, rows), BF16)]
    n_steps = batch * steps
    for src, part, parts in cast:
        part_rows = src.shape[0] // parts
        slab = part_rows // n_steps
        assert slab * n_steps * parts == src.shape[0] and slab % (2 * SUBLANES) == 0
        in_specs.append(pl.BlockSpec((slab, src.shape[1]),
                                     lambda b, i, part=part: (part * n_steps + b * steps + i, 0)))
        out_specs.append(pl.BlockSpec((slab, src.shape[1]), lambda b, i: (b * steps + i, 0)))
        out_shape.append(jax.ShapeDtypeStruct((part_rows, src.shape[1]), BF16))
        args.append(src)
    outs = pl.pallas_call(
        functools.partial(_attn_kernel, n_kv=n_kv, group=group, two=two, n_cast=len(cast)),
        grid=(batch, steps),
        in_specs=in_specs,
        out_specs=out_specs,
        out_shape=out_shape,
        scratch_shapes=[pltpu.VMEM((ATTN_SCORE_SLOTS, n_keys, tq), BF16)],
        compiler_params=_params(("parallel", "arbitrary"), 4 * n_keys * kw * 2 + 4 * tq * n_keys * 4),
        name="attn_two_seg" if two else "attn_one_seg",
    )(*args)
    return outs if cast else outs[0]


def _fourier_kernel(c_ref, s_ref, cn_ref, sn_ref, yc_ref, ys_ref, rev_ref, o_ref):
    yc, ys = yc_ref[...], ys_ref[...]
    a = jnp.dot(c_ref[...], yc, preferred_element_type=F32)
    b = jnp.dot(s_ref[...], ys, preferred_element_type=F32)
    o_ref[0] = (a + b).astype(BF16)
    nxt = (jnp.dot(cn_ref[...], yc, preferred_element_type=F32)
           - jnp.dot(sn_ref[...], ys, preferred_element_type=F32))[0:1]
    mirrored = jnp.dot(rev_ref[...], (a - b).astype(BF16), preferred_element_type=F32)
    row = lax.broadcasted_iota(jnp.int32, mirrored.shape, 0)
    o_ref[1] = jnp.where(row == 0, nxt, mirrored).astype(BF16)


def _fourier(c_tab, s_tab, yc, ys, batch, tm):
    n = c_tab.shape[1]
    w = yc.shape[1]
    nblk = n // 2 // tm
    rev = np.zeros((tm, tm), np.float32)
    rev[np.arange(1, tm), tm - np.arange(1, tm)] = 1.0
    few = 2 * SUBLANES
    next_rows = tm // few
    return pl.pallas_call(
        _fourier_kernel,
        grid=(nblk, batch),
        in_specs=[
            pl.BlockSpec((tm, n), lambda i, b: (i, 0)),
            pl.BlockSpec((tm, n), lambda i, b: (i, 0)),
            pl.BlockSpec((few, n), lambda i, b: ((i + 1) * next_rows, 0)),
            pl.BlockSpec((few, n), lambda i, b: ((i + 1) * next_rows, 0)),
            pl.BlockSpec((n, w), lambda i, b: (b, 0)),
            pl.BlockSpec((n, w), lambda i, b: (b, 0)),
            pl.BlockSpec((tm, tm), lambda i, b: (0, 0)),
        ],
        out_specs=pl.BlockSpec((None, 2, None, tm, w), lambda i, b: (b, 0, i, 0, 0)),
        out_shape=jax.ShapeDtypeStruct((batch, 2, nblk, tm, w), BF16),
        compiler_params=_params(("arbitrary", "arbitrary"), 4 * tm * n * 2 + 4 * n * w * 2),
        name="fourier",
    )(c_tab, s_tab, c_tab, s_tab, yc, ys, jnp.asarray(rev, F32).astype(BF16))


def _dft_tables(n, group_dim):
    scale = 1.0 / math.sqrt(n * group_dim)
    rows = n // 2 + 2 * SUBLANES
    r = int(round(math.sqrt(n)))
    if r * r != n:
        k = jnp.arange(rows, dtype=jnp.int32)
        t = jnp.arange(n, dtype=jnp.int32)
        ang = ((k[:, None] * t[None, :]) % n).astype(F32) * (2.0 * math.pi / n)
        return (jnp.cos(ang) * scale).astype(BF16), (-jnp.sin(ang) * scale).astype(BF16)
    k = jnp.arange(rows, dtype=jnp.int32)[:, None]
    t = jnp.arange(r, dtype=jnp.int32)[None, :]
    ang_a = ((k * t) % r).astype(F32) * (2.0 * math.pi / r)
    ang_b = ((k * t) % n).astype(F32) * (2.0 * math.pi / n)
    ca, sa = jnp.cos(ang_a)[:, :, None], jnp.sin(ang_a)[:, :, None]
    cb, sb = jnp.cos(ang_b)[:, None, :], jnp.sin(ang_b)[:, None, :]
    c = ((ca * cb - sa * sb) * scale).astype(BF16)
    s = ((sa * cb + ca * sb) * (-scale)).astype(BF16)
    return c.reshape(rows, n), s.reshape(rows, n)


def _channel_dft(width, group_dim):
    j = np.arange(group_dim)
    ang = 2.0 * np.pi * ((j[:, None] * j[None, :]) % group_dim) / group_dim
    eye = np.eye(width // group_dim)
    cs = np.concatenate([np.kron(eye, np.cos(ang)), np.kron(eye, np.sin(ang))], axis=1)
    return jnp.asarray(cs, dtype=F32).astype(BF16)


def _na_kernel(ws_ref, var_ref, q_ref, k_ref, v_ref, kc_ref, vc_ref, bias_ref, o_ref, *, band):
    i = pl.program_id(1)
    start = pl.multiple_of(ws_ref[i] * GRID_W, GRID_W)
    var = var_ref[i]
    kw = k_ref[pl.ds(start, band), :]
    vw = v_ref[pl.ds(start, band), :]
    kc, vc = kc_ref[...], vc_ref[...]
    q = q_ref[...]
    dn = (((1,), (1,)), ((), ()))
    head_of_lane = lax.broadcasted_iota(jnp.int32, q.shape, 1) // HEAD_DIM
    out = jnp.zeros(q.shape, F32)
    for h in range(C_HEADS):
        mine = head_of_lane == h
        qh = jnp.where(mine, q, jnp.zeros_like(q))
        s1 = lax.dot_general(qh, kw, dn, preferred_element_type=F32) + bias_ref[var, h].astype(F32)
        s2 = lax.dot_general(qh, kc, dn, preferred_element_type=F32)
        m = jnp.maximum(jnp.max(s1, axis=-1, keepdims=True), jnp.max(s2, axis=-1, keepdims=True))
        p1 = jnp.exp((s1 - m).astype(BF16))
        p2 = jnp.exp((s2 - m).astype(BF16))
        l = jnp.sum(p1.astype(F32), axis=-1, keepdims=True) + jnp.sum(p2.astype(F32), axis=-1, keepdims=True)
        o = jnp.dot(p1, vw, preferred_element_type=F32) + jnp.dot(p2, vc, preferred_element_type=F32)
        out = jnp.where(mine, o * (1.0 / l), out)
    o_ref[...] = out.astype(BF16)


def _na_plan(seq):
    rows = seq // GRID_W
    wr = min(NA_WIN_R, rows)
    wc = min(NA_WIN_C, GRID_W)
    qr = min(NA_Q_ROWS, rows)
    band_rows = min(rows, 2 * ((qr + wr) // 2))
    nblk = rows // qr
    n_row_off, n_col_off = 2 * NA_WIN_R - 1, 2 * NA_WIN_C - 1
    c = np.arange(GRID_W)
    cs = np.clip(c - wc // 2, 0, GRID_W - wc)
    col_ok = (c[None, :] >= cs[:, None]) & (c[None, :] < cs[:, None] + wc)
    col_idx = np.where(col_ok, c[None, :] - c[:, None] + (NA_WIN_C - 1), n_col_off).astype(np.int32)
    ws_list, var_list, variants = [], [], []
    for blk in range(nblk):
        r = blk * qr + np.arange(qr)
        rs = np.clip(r - wr // 2, 0, rows - wr)
        ws = int(np.clip(rs[0], 0, rows - band_rows))
        kr = ws + np.arange(band_rows)
        row_ok = (kr[None, :] >= rs[:, None]) & (kr[None, :] < rs[:, None] + wr)
        assert row_ok.sum(axis=1).min() == wr
        idx = np.where(row_ok, kr[None, :] - r[:, None] + (NA_WIN_R - 1), n_row_off).astype(np.int32)
        for v, known in enumerate(variants):
            if np.array_equal(known, idx):
                break
        else:
            v = len(variants)
            variants.append(idx)
        ws_list.append(ws)
        var_list.append(v)
    return (np.asarray(ws_list, np.int32), np.asarray(var_list, np.int32),
            np.stack(variants), col_idx, qr, band_rows)


def _natten(qn, kn, vn, kc, vc, rel_bias_l, batch):
    rows, w = qn.shape
    seq = rows // batch
    ctx_len = kc.shape[0] // batch
    ws, var, row_idx, col_idx, qr, band_rows = _na_plan(seq)
    nblk = ws.shape[0]
    tq, band = qr * GRID_W, band_rows * GRID_W
    nvar = row_idx.shape[0]
    padded = jnp.pad(rel_bias_l, ((0, 0), (0, 1), (0, 1)), constant_values=NEG_BIG)
    tiles = padded[:, :, col_idx].astype(BF16)
    bias = jnp.take(tiles, jnp.asarray(row_idx.reshape(-1)), axis=1)
    bias = bias.reshape(C_HEADS, nvar, qr, band_rows, GRID_W, GRID_W)
    bias = jnp.transpose(bias, (1, 0, 2, 4, 3, 5)).reshape(nvar, C_HEADS, tq, band)
    grid_spec = pltpu.PrefetchScalarGridSpec(
        num_scalar_prefetch=2,
        grid=(batch, nblk),
        in_specs=[
            pl.BlockSpec((tq, w), lambda b, i, ws_r, var_r: (b * nblk + i, 0)),
            pl.BlockSpec((seq, w), lambda b, i, ws_r, var_r: (b, 0)),
            pl.BlockSpec((seq, w), lambda b, i, ws_r, var_r: (b, 0)),
            pl.BlockSpec((ctx_len, w), lambda b, i, ws_r, var_r: (b, 0)),
            pl.BlockSpec((ctx_len, w), lambda b, i, ws_r, var_r: (b, 0)),
            pl.BlockSpec((nvar, C_HEADS, tq, band), lambda b, i, ws_r, var_r: (0, 0, 0, 0)),
        ],
        out_specs=pl.BlockSpec((tq, w), lambda b, i, ws_r, var_r: (b * nblk + i, 0)),
    )
    nbytes = 2 * nvar * C_HEADS * tq * band * 2 + 4 * seq * w * 2 + 6 * tq * (band + ctx_len) * 4
    return pl.pallas_call(
        functools.partial(_na_kernel, band=band),
        grid_spec=grid_spec,
        out_shape=jax.ShapeDtypeStruct((rows, w), BF16),
        compiler_params=_params(("parallel", "arbitrary"), nbytes),
        name="natten",
    )(jnp.asarray(ws), jnp.asarray(var), qn, kn, vn, kc, vc, bias)


def _merge_kernel(oa_ref, ob_ref, oc_ref, x_ref, gate_ref, shift_ref, scale_ref, ga_ref, gb_ref, gc_ref,
                  w_ref, gf_ref, wa_ref, wb_ref, xo_ref, h_ref, lg_ref, *, c_feature_major):
    def normed(o_ref, g_ref):
        return _rms(o_ref[...].astype(F32), g_ref[...]).astype(BF16)

    def normed_t(o_ref, g_ref):
        o = o_ref[...].astype(F32)
        return (o * lax.rsqrt(jnp.mean(o * o, axis=0, keepdims=True) + EPS) * g_ref[...]).astype(BF16)

    dn_t = (((0,), (0,)), ((), ()))
    a_w, b_w = oa_ref.shape[0], ob_ref.shape[1]
    y = lax.dot_general(normed_t(oa_ref, ga_ref), w_ref[0:a_w, :], dn_t, preferred_element_type=F32)
    y = y + jnp.dot(normed(ob_ref, gb_ref), w_ref[a_w:a_w + b_w, :], preferred_element_type=F32)
    if c_feature_major:
        y = y + lax.dot_general(normed_t(oc_ref, gc_ref), w_ref[a_w + b_w:, :], dn_t,
                                preferred_element_type=F32)
    else:
        y = y + jnp.dot(normed(oc_ref, gc_ref), w_ref[a_w + b_w:, :], preferred_element_type=F32)
    x = x_ref[...] + gate_ref[...] * y
    xo_ref[...] = x
    h = _rms(x, gf_ref[...]) * (1.0 + scale_ref[...]) + shift_ref[...]
    _store_row_tiled(h_ref, h)
    h_hi = h.astype(BF16)
    h_lo = (h - h_hi.astype(F32)).astype(BF16)
    t = jnp.dot(h_hi, wa_ref[...], preferred_element_type=F32) + jnp.dot(h_lo, wb_ref[...], preferred_element_type=F32)
    t = jnp.transpose(t)
    lg_ref[...] = t[0:N_EXPERTS] + t[N_EXPERTS:2 * N_EXPERTS]


def _router_split(w_router_l):
    d, n_exp = w_router_l.shape
    w_hi = w_router_l.astype(BF16)
    w_lo = (w_router_l - w_hi.astype(F32)).astype(BF16)
    wa = jnp.concatenate([w_hi, w_lo, jnp.zeros((d, LANES - 2 * n_exp), BF16)], axis=1)
    wb = jnp.concatenate([w_hi, jnp.zeros((d, LANES - n_exp), BF16)], axis=1)
    return wa, wb


def _merge(oa_t, ob, oc, c_feature_major, x2, mod, mod_row, g_a, g_b, g_c, w_out_bf, g_ffn, w_router_ab, tm):
    rows, d = x2.shape
    a_w, b_w = oa_t.shape[0], ob.shape[-1]
    nblk = ob.shape[2]
    assert ob.shape[3] == tm or nblk == 1

    def ob_index(i):
        tile = i % (2 * nblk)
        return (i // (2 * nblk), tile // nblk, jnp.where(tile < nblk, tile, 2 * nblk - 1 - tile), 0, 0)

    if nblk == 1 and ob.shape[3] != tm:
        ob = ob.reshape(-1, b_w)
        ob_spec = pl.BlockSpec((tm, b_w), lambda i: (i, 0))
    else:
        ob_spec = pl.BlockSpec((None, None, None, tm, b_w), ob_index)
    row = lambda i: (i, 0)
    col = lambda i: (0, i)
    const = lambda i: (0, 0)
    mod_spec = lambda j: pl.BlockSpec((None, None, 1, d), lambda i: (mod_row(i), j, 0, 0))
    if c_feature_major:
        c_w = oc.shape[0]
        oc_spec, gc_spec, g_c2 = pl.BlockSpec((c_w, tm), col), pl.BlockSpec((c_w, 1), const), g_c.reshape(-1, 1)
    else:
        c_w = oc.shape[1]
        oc_spec, gc_spec, g_c2 = pl.BlockSpec((tm, c_w), row), pl.BlockSpec((1, c_w), const), g_c.reshape(1, -1)
    return pl.pallas_call(
        functools.partial(_merge_kernel, c_feature_major=c_feature_major),
        grid=(rows // tm,),
        in_specs=[
            pl.BlockSpec((a_w, tm), col), ob_spec, oc_spec,
            pl.BlockSpec((tm, d), row),
            mod_spec(2), mod_spec(3), mod_spec(4),
            pl.BlockSpec((a_w, 1), const), pl.BlockSpec((1, b_w), const), gc_spec,
            pl.BlockSpec((a_w + b_w + c_w, d), const),
            pl.BlockSpec((1, d), const),
            pl.BlockSpec((d, LANES), const),
            pl.BlockSpec((d, LANES), const),
        ],
        out_specs=[pl.BlockSpec((tm, d), row),
                   pl.BlockSpec(_row_tiled_shape(tm, d), row),
                   pl.BlockSpec((N_EXPERTS, tm), lambda i: (0, i))],
        out_shape=[jax.ShapeDtypeStruct((rows, d), F32), jax.ShapeDtypeStruct(_row_tiled_shape(rows, d), F32),
                   jax.ShapeDtypeStruct((N_EXPERTS, rows), F32)],
        compiler_params=_params(("parallel",), 2 * d * d * 2 + 8 * tm * d * 4),
        name="merge_out",
    )(oa_t, ob, oc, x2, mod, mod, mod, g_a.reshape(-1, 1), g_b.reshape(1, -1), g_c2,
      w_out_bf, g_ffn.reshape(1, d), *w_router_ab)


def _route_kernel(lg_ref, tri_ref, idx_ref, gate_ref, *, n, cap):
    lg = lg_ref[...]
    e = jnp.exp(lg - jnp.max(lg, axis=0, keepdims=True))
    aff = e / jnp.sum(e, axis=0, keepdims=True)
    fcap = float(cap)

    def enough(v):
        return jnp.sum(jnp.where(aff >= v, 1.0, 0.0), axis=1, keepdims=True) >= fcap

    p = jnp.full((N_EXPERTS, 1), 2.0, F32)
    for j in (64, 32, 16, 8, 4, 2, 1):
        cand = p * (2.0 ** -j)
        p = jnp.where(enough(cand), p, cand)
    base = p * 0.5
    mant = jnp.zeros((N_EXPERTS, 1), F32)
    for j in range(22, -1, -1):
        cand = mant + float(2 ** j)
        mant = jnp.where(enough(base * (1.0 + cand * (2.0 ** -23))), cand, mant)
    thr = base * (1.0 + mant * (2.0 ** -23))

    tri = tri_ref[...]

    def excl_cumsum(mask):
        out, off = [], jnp.zeros((N_EXPERTS, 1), F32)
        for c in range(n // LANES):
            mc = mask[:, c * LANES:(c + 1) * LANES]
            inc = jnp.dot(mc.astype(BF16), tri, preferred_element_type=F32)
            out.append(inc - mc + off)
            off = off + inc[:, LANES - 1:LANES]
        return jnp.concatenate(out, axis=1), off

    gt = jnp.where(aff > thr, 1.0, 0.0)
    eq = jnp.where(aff == thr, 1.0, 0.0)
    need = fcap - jnp.sum(gt, axis=1, keepdims=True)
    eq_rank, _ = excl_cumsum(eq)
    sel = jnp.maximum(gt, jnp.where(eq_rank < need, eq, 0.0))
    pos, _ = excl_cumsum(sel)

    chosen = sel > 0.0
    tok = lax.broadcasted_iota(jnp.int32, (N_EXPERTS, n), 1)
    dist = jnp.where(chosen, tok - pos.astype(jnp.int32), 0)
    tok1 = jnp.where(chosen, tok + 1, 0)
    gate = jnp.where(chosen, aff, 0.0)
    for bit in range((n - 1).bit_length()):
        left = n - (1 << bit)
        dist_in, tok_in, gate_in = pltpu.roll(dist, left, 1), pltpu.roll(tok1, left, 1), pltpu.roll(gate, left, 1)
        arrives = ((dist_in >> bit) & 1) == 1
        stays = ((dist >> bit) & 1) == 0
        tok1 = jnp.where(arrives, tok_in, jnp.where(stays, tok1, 0))
        gate = jnp.where(arrives, gate_in, jnp.where(stays, gate, 0.0))
        dist = jnp.where(arrives, dist_in, jnp.where(stays, dist, 0))
    idx_ref[...] = tok1[:, :cap] - 1
    gate_ref[...] = gate[:, :cap]


def _route(logits_t, batch, cap):
    n = logits_t.shape[1] // batch
    assert n & (n - 1) == 0 and n % LANES == 0
    tri = jnp.asarray(np.triu(np.ones((LANES, LANES))), dtype=BF16)
    return pl.pallas_call(
        functools.partial(_route_kernel, n=n, cap=cap),
        grid=(batch,),
        in_specs=[pl.BlockSpec((N_EXPERTS, n), lambda b: (0, b)),
                  pl.BlockSpec((LANES, LANES), lambda b: (0, 0))],
        out_specs=[pl.BlockSpec((None, N_EXPERTS, cap), lambda b: (b, 0, 0)),
                   pl.BlockSpec((None, N_EXPERTS, cap), lambda b: (b, 0, 0))],
        out_shape=[jax.ShapeDtypeStruct((batch, N_EXPERTS, cap), jnp.int32),
                   jax.ShapeDtypeStruct((batch, N_EXPERTS, cap), F32)],
        compiler_params=_params(("parallel",), 64 * N_EXPERTS * n * 4),
        name="route_topk",
    )(logits_t, tri)


def _gather_kernel(idx_ref, h_ref, xe_ref, rows_scr, *, cap, chunks):
    b, ex = pl.program_id(0), pl.program_id(1)
    base = (b * N_EXPERTS + ex) * cap
    unroll = 2 * SUBLANES

    def body(i, carry):
        rows = [h_ref[_token_tile(idx_ref[base + i * unroll + u], chunks), :] for u in range(unroll)]
        for u in range(unroll):
            rows_scr[_token_tile(i * unroll + u, chunks), :] = rows[u]
        return carry

    lax.fori_loop(0, cap // unroll, body, 0)
    xe_ref[...] = _load_row_tiled(rows_scr, chunks).astype(BF16)


def _gather(idx, h3, d, batch, cap):
    chunks = d // LANES
    n = h3.shape[0] // chunks // batch
    grid_spec = pltpu.PrefetchScalarGridSpec(
        num_scalar_prefetch=1,
        grid=(batch, N_EXPERTS),
        in_specs=[pl.BlockSpec(_row_tiled_shape(n, d), lambda b, ex, idx_r: (b, 0))],
        out_specs=pl.BlockSpec((None, cap, d), lambda b, ex, idx_r: (ex, b, 0)),
        scratch_shapes=[pltpu.VMEM(_row_tiled_shape(cap, d), F32)],
    )
    return pl.pallas_call(
        functools.partial(_gather_kernel, cap=cap, chunks=chunks),
        grid_spec=grid_spec,
        out_shape=jax.ShapeDtypeStruct((N_EXPERTS, batch * cap, d), BF16),
        compiler_params=_params(("arbitrary", "arbitrary"), 2 * n * d * 4 + 3 * cap * d * 4),
        name="moe_gather",
    )(idx.reshape(-1), h3)


def _ffn_kernel(x_ref, wg_ref, wu_ref, wd_ref, y_ref, *, f_chunk):
    x = x_ref[...]
    y = None
    for c in range(wg_ref.shape[1] // f_chunk):
        cs = slice(c * f_chunk, (c + 1) * f_chunk)
        g = jnp.dot(x, wg_ref[:, cs], preferred_element_type=F32)
        u = jnp.dot(x, wu_ref[:, cs], preferred_element_type=F32)
        a = (g / (1.0 + jnp.exp(-g)) * u).astype(BF16)
        part = jnp.dot(a, wd_ref[cs, :], preferred_element_type=F32)
        y = part if y is None else y + part
    _store_row_tiled(y_ref, y)


def _expert_ffn(xe, layer, wg_bf, wu_bf, wd_bf, tm):
    n_exp, rows, d = xe.shape
    ff = wg_bf.shape[3]
    tm = min(tm, rows)
    return pl.pallas_call(
        functools.partial(_ffn_kernel, f_chunk=min(FFN_F_CHUNK, ff)),
        grid=(n_exp, rows // tm),
        in_specs=[
            pl.BlockSpec((None, tm, d), lambda ex, i: (ex, i, 0)),
            pl.BlockSpec((None, None, d, ff), lambda ex, i: (layer, ex, 0, 0)),
            pl.BlockSpec((None, None, d, ff), lambda ex, i: (layer, ex, 0, 0)),
            pl.BlockSpec((None, None, ff, d), lambda ex, i: (layer, ex, 0, 0)),
        ],
        out_specs=pl.BlockSpec((None,) + _row_tiled_shape(tm, d), lambda ex, i: (ex, i, 0)),
        out_shape=jax.ShapeDtypeStruct((n_exp,) + _row_tiled_shape(rows, d), F32),
        compiler_params=_params(("parallel", "arbitrary"), 2 * 3 * d * ff * 2 + 8 * tm * d * 4),
        name="expert_ffn",
    )(xe, wg_bf, wu_bf, wd_bf)


def _combine_kernel(idx_ref, gate_ref, y_ref, acc_ref, *, cap, chunks, experts_per_step):
    b, step = pl.program_id(0), pl.program_id(1)

    @pl.when(step == 0)
    def _():
        acc_ref[...] = jnp.zeros_like(acc_ref)

    unroll = SUBLANES
    for e in range(experts_per_step):
        base = (b * N_EXPERTS + step * experts_per_step + e) * cap

        def body(i, carry, e=e, base=base):
            js = [i * unroll + u for u in range(unroll)]
            rs = [_token_tile(idx_ref[base + j], chunks) for j in js]
            new = [acc_ref[r, :] + y_ref[e, _token_tile(j, chunks), :] * gate_ref[base + j]
                   for r, j in zip(rs, js)]
            for r, v in zip(rs, new):
                acc_ref[r, :] = v
            return carry

        lax.fori_loop(0, cap // unroll, body, 0)


def _combine(idx, gate, y3, d, batch, n, cap):
    chunks = d // LANES
    experts_per_step = N_EXPERTS if cap * N_EXPERTS <= COMBINE_ROWS_PER_STEP else 1
    grid_spec = pltpu.PrefetchScalarGridSpec(
        num_scalar_prefetch=2,
        grid=(batch, N_EXPERTS // experts_per_step),
        in_specs=[pl.BlockSpec((experts_per_step,) + _row_tiled_shape(cap, d),
                               lambda b, s, idx_r, gate_r: (s, b, 0))],
        out_specs=pl.BlockSpec(_row_tiled_shape(n, d), lambda b, s, idx_r, gate_r: (b, 0)),
    )
    return pl.pallas_call(
        functools.partial(_combine_kernel, cap=cap, chunks=chunks, experts_per_step=experts_per_step),
        grid_spec=grid_spec,
        out_shape=jax.ShapeDtypeStruct(_row_tiled_shape(batch * n, d), F32),
        compiler_params=_params(("arbitrary", "arbitrary"), 2 * n * d * 4 + 2 * cap * d * 4),
        name="moe_combine",
    )(idx.reshape(-1), gate.reshape(-1), y3)


def _resid_kernel(x_ref, m_ref, gate_ref, gf_ref, o_ref, *, final):
    x = x_ref[...] + gate_ref[...] * _load_row_tiled(m_ref, x_ref.shape[1] // LANES)
    o_ref[...] = _rms(x, gf_ref[...]) if final else x


def _residual(x2, moe3, mod, mod_row, g_final, final, tm):
    rows, d = x2.shape
    row = lambda i: (i, 0)
    return pl.pallas_call(
        functools.partial(_resid_kernel, final=final),
        grid=(rows // tm,),
        in_specs=[pl.BlockSpec((tm, d), row), pl.BlockSpec(_row_tiled_shape(tm, d), row),
                  pl.BlockSpec((None, None, 1, d), lambda i: (mod_row(i), 5, 0, 0)),
                  pl.BlockSpec((1, d), lambda i: (0, 0))],
        out_specs=pl.BlockSpec((tm, d), row),
        out_shape=jax.ShapeDtypeStruct((rows, d), F32),
        compiler_params=_params(("parallel",), 6 * tm * d * 4),
        name="moe_residual_final" if final else "moe_residual",
    )(x2, moe3, mod, g_final.reshape(1, d))


def _moe(h3, logits_t, batch, layer, wg_bf, wu_bf, wd_bf):
    d = wg_bf.shape[2]
    n = logits_t.shape[1] // batch
    cap = max(1, CAPACITY_FACTOR * n // N_EXPERTS)
    idx, gate = _route(logits_t, batch, cap)
    xe = _gather(idx, h3, d, batch, cap)
    y3 = _expert_ffn(xe, layer, wg_bf, wu_bf, wd_bf, FFN_ROW_TILE)
    return _combine(idx, gate, y3, d, batch, n, cap)


def _rope_tables(n):
    t = np.arange(n)
    n_freq = HEAD_DIM // 4
    inv = ROPE_THETA ** (-np.arange(n_freq, dtype=np.float64) / n_freq)
    ang_r = (t // GRID_W)[:, None] * inv[None, :]
    ang_c = (t % GRID_W)[:, None] * inv[None, :]
    ang = np.concatenate([ang_r, ang_r, ang_c, ang_c], axis=1)
    sign = np.concatenate([-np.ones(n_freq), np.ones(n_freq)] * 2)[None, :]
    reps = LANES // HEAD_DIM
    cos = np.tile(np.cos(ang), (1, reps))
    sin = np.tile(np.sin(ang) * sign, (1, reps))
    return jnp.asarray(cos, F32), jnp.asarray(sin, F32)


def kernel(x, c, ctx, c_ctx, w_ada, b_ada, g_mix, g_ffn, w_in, g_q, g_k, rel_bias, g_out_a, g_out_b, g_out_c, w_out, w_router, w_gate, w_up, w_down, g_final):
    batch, seq, d = x.shape
    ctx_len = ctx.shape[1]
    depth = w_ada.shape[0]
    b_w = g_out_b.shape[1]
    group_dim = b_w // B_GROUPS
    ctx_row = batch
    assert batch < MOD_ROWS and seq % ROW_TILE == 0

    cvec = jnp.zeros((MOD_ROWS, d), F32).at[:batch].set(c).at[ctx_row].set(c_ctx)
    mod_all = _ada(cvec, w_ada, b_ada).reshape(depth, MOD_ROWS, 6, 1, d)

    rope_tabs = _rope_tables(seq)
    blk = np.arange(LANES) // HEAD_DIM
    ones_hd = jnp.asarray(blk[:, None] == blk[None, :], dtype=BF16)
    cs = _channel_dft(b_w, group_dim)
    dft_lat = _dft_tables(seq, group_dim)
    dft_ctx = _dft_tables(ctx_len, group_dim)

    lat_tile = ROW_TILE
    ctx_tile = min(ROW_TILE, ctx_len)
    lat_row = lambda i: i // (seq // lat_tile)
    ctx_mod_row = lambda i: ctx_row
    group_a = A_Q_HEADS // A_KV_HEADS

    n_exp, ff = w_gate.shape[1], w_gate.shape[3]
    w_stacks = (w_gate.reshape(-1, ff), w_up.reshape(-1, ff), w_down.reshape(-1, d))
    xl = x.reshape(batch * seq, d)
    xc = ctx.reshape(batch * ctx_len, d)
    for l in range(depth):
        last = l == depth - 1
        mod = mod_all[l]
        w_in_bf = w_in[l].astype(BF16)
        w_out_bf = w_out[l].astype(BF16)
        w_router_t = _router_split(w_router[l])
        gq_t = jnp.tile(g_q[l], LANES // HEAD_DIM).reshape(1, LANES)
        gk_t = jnp.tile(g_k[l], LANES // HEAD_DIM).reshape(1, LANES)

        qa, ka, va_t, yc, ys, qn, kn, vn = _inproj(xl, mod, lat_row, g_mix[l], w_in_bf, gq_t, gk_t, ones_hd, cs,
                                                   rope_tabs, seq, lat_tile)
        qa_c, ka_c, va_c_t, yc_c, ys_c, qn_c, kn_c, vn_c, vn_c_t = _inproj(
            xc, mod, ctx_mod_row, g_mix[l], w_in_bf, gq_t, gk_t, ones_hd, cs, None, ctx_len, ctx_tile)
        o_a_t, wg_bf, wu_bf, wd_bf = _attention(qa, ka, va_t, ka_c, va_c_t, batch, A_KV_HEADS, group_a, ATTN_Q_TILE,
                                                cast=[(w, l, depth) for w in w_stacks])
        wg_bf, wu_bf = wg_bf.reshape(1, n_exp, d, ff), wu_bf.reshape(1, n_exp, d, ff)
        wd_bf = wd_bf.reshape(1, n_exp, ff, d)
        o_b = _fourier(dft_lat[0], dft_lat[1], yc, ys, batch, min(FOURIER_ROW_TILE, seq // 2))
        o_c = _natten(qn, kn, vn, kn_c, vn_c, rel_bias[l], batch)
        x_mid, h2, logits_t = _merge(o_a_t, o_b, o_c, False, xl, mod, lat_row, g_out_a[l], g_out_b[l],
                                     g_out_c[l], w_out_bf, g_ffn[l], w_router_t, lat_tile)
        moe = _moe(h2, logits_t, batch, 0, wg_bf, wu_bf, wd_bf)
        xl = _residual(x_mid, moe, mod, lat_row, g_final, last, lat_tile)
        if not last:
            o_a_c_t = _attention(qa_c, ka_c, va_c_t, None, None, batch, A_KV_HEADS, group_a, ctx_len)
            o_b_c = _fourier(dft_ctx[0], dft_ctx[1], yc_c, ys_c, batch, ctx_len // 2)
            o_c_c_t = _attention(qn_c, kn_c, vn_c_t, None, None, batch, C_HEADS, 1, ctx_len)
            xc_mid, h2_c, logits_c = _merge(o_a_c_t, o_b_c, o_c_c_t, True, xc, mod, ctx_mod_row, g_out_a[l],
                                            g_out_b[l], g_out_c[l], w_out_bf, g_ffn[l], w_router_t, ctx_tile)
            moe_c = _moe(h2_c, logits_c, batch, 0, wg_bf, wu_bf, wd_bf)
            xc = _residual(xc_mid, moe_c, mod, ctx_mod_row, g_final, False, ctx_tile)
    return xl.reshape(batch, seq, d)
```
